```python
import math
import jax, jax.numpy as jnp
from jax import lax
import numpy as np

D_MODEL = 1024
BATCH = 1
SEQ = 16384
DEPTH = 1
DEC_BATCH = 128
DEC_SEQ = 4
PAST_LEN = 8192
PAGE_SIZE = 128

HEAD_DIM = 64
V_DIM = 2 * HEAD_DIM
N_HEADS = (D_MODEL // 2) // V_DIM
D_ATTN = N_HEADS * V_DIM
QK_WIDTH = N_HEADS * 2 * HEAD_DIM
CONV_CH = D_MODEL - D_ATTN
CONV_WIDTH = 31
D_MIX = D_ATTN + CONV_CH
IN_COLS = 2 * QK_WIDTH + D_ATTN + 2 * CONV_CH
N_GROUPS = 4
EXPERTS_PER_GROUP = 8
N_EXPERTS = N_GROUPS * EXPERTS_PER_GROUP
TOP_K = 2
D_EXPERT = D_MODEL // 4
ROPE_THETA = 10000.0
Q_BLOCK = 128
EPS = 1e-6
N_MOD = 6
NEG_INF = -1e30

kernel_name = 'hymba_diffattn_conformer_hmoe_adaln_step'


def rms_norm(x, g):
    xf = x.astype(jnp.float32)
    y = xf * lax.rsqrt(jnp.mean(xf * xf, axis=-1, keepdims=True) + EPS)
    return (y * g.astype(jnp.float32)).astype(x.dtype)


def layer_norm(x, g, b):
    xf = x.astype(jnp.float32)
    mu = jnp.mean(xf, axis=-1, keepdims=True)
    var = jnp.mean(jnp.square(xf - mu), axis=-1, keepdims=True)
    y = (xf - mu) * lax.rsqrt(var + EPS) * g.astype(jnp.float32) + b.astype(jnp.float32)
    return y.astype(x.dtype)


def lambda_init(layer):
    return 0.8 - 0.6 * math.exp(-0.3 * layer)


def rope(x, pos):
    inv = 1.0 / (ROPE_THETA ** (jnp.arange(0, HEAD_DIM, 2, dtype=jnp.float32) / HEAD_DIM))
    ang = pos.astype(jnp.float32)[:, None] * inv[None, :]
    ang = jnp.concatenate([ang, ang], axis=-1)[:, None, None, :]
    xf = x.astype(jnp.float32)
    x1, x2 = jnp.split(xf, 2, axis=-1)
    rot = jnp.concatenate([-x2, x1], axis=-1)
    return (xf * jnp.cos(ang) + rot * jnp.sin(ang)).astype(x.dtype)


def project(h, w_in, g_q, g_k, pos):
    B, S, _ = h.shape
    proj = h @ w_in
    q, k, v, u = jnp.split(proj, [QK_WIDTH, 2 * QK_WIDTH, 2 * QK_WIDTH + D_ATTN], axis=-1)
    q = rope(rms_norm(q.reshape(B, S, N_HEADS, 2, HEAD_DIM), g_q), pos)
    k = rope(rms_norm(k.reshape(B, S, N_HEADS, 2, HEAD_DIM), g_k), pos)
    v = v.reshape(B, S, N_HEADS, V_DIM)
    a, gate = jnp.split(u, 2, axis=-1)
    glu = a * jax.nn.sigmoid(gate)
    return q, k, v, glu


def diff_attend(q, k, v, q_pos, k_pos, lam):
    s = jnp.einsum('bqhcd,bkhcd->bhcqk', q, k, preferred_element_type=jnp.float32) * (HEAD_DIM ** -0.5)
    s = jnp.where(k_pos[None, :] <= q_pos[:, None], s, NEG_INF)
    p = jax.nn.softmax(s, axis=-1)
    a = p[:, :, 0] - lam * p[:, :, 1]
    return jnp.einsum('bhqk,bkhv->bqhv', a.astype(v.dtype), v)


def prompt_attention(q, k, v, lam):
    B, S = q.shape[0], q.shape[1]
    k_pos = jnp.arange(S)

    def block(i):
        qb = lax.dynamic_slice_in_dim(q, i * Q_BLOCK, Q_BLOCK, axis=1)
        q_pos = i * Q_BLOCK + jnp.arange(Q_BLOCK)
        return diff_attend(qb, k, v, q_pos, k_pos, lam)

    o = lax.map(block, jnp.arange(S // Q_BLOCK))
    return jnp.moveaxis(o, 0, 1).reshape(B, S, N_HEADS, V_DIM)


def sample_attention(q, k, v, lam, cache_k, cache_v, page_table):
    past = page_table.shape[1] * PAGE_SIZE
    n_new = q.shape[1]
    k_pos = jnp.arange(past + n_new)
    q_pos = past + jnp.arange(n_new)

    def one(args):
        qs, ks, vs, pages = args
        kp = cache_k[pages].reshape(past, N_HEADS, 2, HEAD_DIM)
        vp = cache_v[pages].reshape(past, N_HEADS, V_DIM)
        ka = jnp.concatenate([kp, ks], axis=0)[None]
        va = jnp.concatenate([vp, vs], axis=0)[None]
        return diff_attend(qs[None], ka, va, q_pos, k_pos, lam)[0]

    return lax.map(one, (q, k, v, page_table))


def conformer_conv(glu, conv_buf, w_dw, b_dw, g_ln, b_ln):
    xp = jnp.concatenate([conv_buf, glu], axis=1)
    y = lax.conv_general_dilated(xp, w_dw[:, None, :], (1,), 'VALID',
                                 dimension_numbers=('NWC', 'WIO', 'NWC'),
                                 feature_group_count=CONV_CH) + b_dw
    y = jax.nn.silu(layer_norm(y, g_ln, b_ln))
    return y, xp[:, -(CONV_WIDTH - 1):]


def hier_moe(h, w_rg, b_rg, w_re, b_re, w_gate, w_up, w_down):
    B, S, D = h.shape
    t = h.reshape(B * S, D)
    pg = jax.nn.softmax((t @ w_rg + b_rg).astype(jnp.float32), axis=-1)
    pg_top, g_idx = lax.top_k(pg, 1)
    fine = (t @ w_re + b_re).astype(jnp.float32).reshape(-1, N_GROUPS, EXPERTS_PER_GROUP)
    fine = jnp.take_along_axis(fine, g_idx[:, :, None], axis=1)[:, 0]
    pe_top, e_idx = lax.top_k(jax.nn.softmax(fine, axis=-1), TOP_K)
    w = pg_top * pe_top / jnp.sum(pe_top, axis=-1, keepdims=True)
    expert = g_idx * EXPERTS_PER_GROUP + e_idx
    gates = jnp.sum(jax.nn.one_hot(expert, N_EXPERTS, dtype=jnp.float32) * w[..., None], axis=1).astype(t.dtype)
    out = jnp.zeros_like(t)
    for e in range(N_EXPERTS):
        he = jax.nn.silu(t @ w_gate[e]) * (t @ w_up[e])
        out = out + gates[:, e:e + 1] * (he @ w_down[e])
    return out.reshape(B, S, D)


def decoder_layer(x, c, pos, conv_buf, attend, lam_init, p):
    B, S, _ = x.shape
    mod = (jax.nn.silu(c) @ p['w_ada'] + p['b_ada'])[:, None, :]
    sh1, sc1, gt1, sh2, sc2, gt2 = jnp.split(mod, N_MOD, axis=-1)
    h = rms_norm(x, p['g_norm1']) * (1 + sc1) + sh1
    q, k, v, glu = project(h, p['w_in'], p['g_qnorm'], p['g_knorm'], pos)
    f32 = jnp.float32
    lam = (jnp.exp(jnp.sum(p['lam_q1'].astype(f32) * p['lam_k1'].astype(f32)))
           - jnp.exp(jnp.sum(p['lam_q2'].astype(f32) * p['lam_k2'].astype(f32))) + lam_init)
    o = attend(q, k, v, lam)
    o_attn = (rms_norm(o, p['g_subln']) * (1.0 - lam_init)).reshape(B, S, D_ATTN)
    o_conv, new_buf = conformer_conv(glu, conv_buf, p['w_dw'], p['b_dw'], p['g_ln_conv'], p['b_ln_conv'])
    x = x + gt1 * (jnp.concatenate([o_attn, o_conv], axis=-1) @ p['w_out'])
    h2 = rms_norm(x, p['g_norm2']) * (1 + sc2) + sh2
    x = x + gt2 * hier_moe(h2, p['w_router_group'], p['b_router_group'], p['w_router_expert'],
                           p['b_router_expert'], p['w_gate_e'], p['w_up_e'], p['w_down_e'])
    return x, k, v, new_buf


def setup_inputs(seed: int = 0) -> dict:
    key = jax.random.key(seed)
    ks = jax.random.split(key, 40)
    n_pages = PAST_LEN // PAGE_SIZE
    n_pool = (DEC_BATCH * n_pages * 5) // 4
    nrm = jax.random.normal
    f32 = jnp.float32
    page_table = jax.random.permutation(ks[0], n_pool)[:DEC_BATCH * n_pages].reshape(DEC_BATCH, n_pages).astype(jnp.int32)
    return {
        'x_prompt': nrm(ks[1], (BATCH, SEQ, D_MODEL), f32),
        'x_sample': nrm(ks[2], (DEC_BATCH, DEC_SEQ, D_MODEL), f32),
        'cache_k': nrm(ks[3], (DEPTH, n_pool, PAGE_SIZE, N_HEADS, 2, HEAD_DIM), f32),
        'cache_v': nrm(ks[4], (DEPTH, n_pool, PAGE_SIZE, N_HEADS, V_DIM), f32),
        'state_conv': nrm(ks[5], (DEPTH, DEC_BATCH, CONV_WIDTH - 1, CONV_CH), f32),
        'page_table': page_table,
        'c_prompt': nrm(ks[6], (BATCH, D_MODEL), f32),
        'c_sample': nrm(ks[7], (DEC_BATCH, D_MODEL), f32),
        'w_ada': nrm(ks[8], (DEPTH, D_MODEL, N_MOD * D_MODEL), f32) * D_MODEL ** -0.5,
        'b_ada': 0.02 * nrm(ks[9], (DEPTH, N_MOD * D_MODEL), f32),
        'g_norm1': 1.0 + 0.02 * nrm(ks[10], (DEPTH, D_MODEL), f32),
        'g_norm2': 1.0 + 0.02 * nrm(ks[11], (DEPTH, D_MODEL), f32),
        'w_in': nrm(ks[12], (DEPTH, D_MODEL, IN_COLS), f32) * D_MODEL ** -0.5,
        'g_qnorm': 1.0 + 0.02 * nrm(ks[13], (DEPTH, HEAD_DIM), f32),
        'g_knorm': 1.0 + 0.02 * nrm(ks[14], (DEPTH, HEAD_DIM), f32),
        'lam_q1': 0.1 * nrm(ks[15], (DEPTH, HEAD_DIM), f32),
        'lam_k1': 0.1 * nrm(ks[16], (DEPTH, HEAD_DIM), f32),
        'lam_q2': 0.1 * nrm(ks[17], (DEPTH, HEAD_DIM), f32),
        'lam_k2': 0.1 * nrm(ks[18], (DEPTH, HEAD_DIM), f32),
        'g_subln': 1.0 + 0.02 * nrm(ks[19], (DEPTH, V_DIM), f32),
        'w_dw': nrm(ks[20], (DEPTH, CONV_WIDTH, CONV_CH), f32) * CONV_WIDTH ** -0.5,
        'b_dw': 0.02 * nrm(ks[21], (DEPTH, CONV_CH), f32),
        'g_ln_conv': 1.0 + 0.02 * nrm(ks[22], (DEPTH, CONV_CH), f32),
        'b_ln_conv': 0.02 * nrm(ks[23], (DEPTH, CONV_CH), f32),
        'w_out': nrm(ks[24], (DEPTH, D_MIX, D_MODEL), f32) * D_MIX ** -0.5,
        'w_router_group': nrm(ks[25], (DEPTH, D_MODEL, N_GROUPS), f32) * D_MODEL ** -0.5,
        'b_router_group': 0.01 * nrm(ks[26], (DEPTH, N_GROUPS), f32),
        'w_router_expert': nrm(ks[27], (DEPTH, D_MODEL, N_EXPERTS), f32) * D_MODEL ** -0.5,
        'b_router_expert': 0.01 * nrm(ks[28], (DEPTH, N_EXPERTS), f32),
        'w_gate_e': nrm(ks[29], (DEPTH, N_EXPERTS, D_MODEL, D_EXPERT), f32) * D_MODEL ** -0.5,
        'w_up_e': nrm(ks[30], (DEPTH, N_EXPERTS, D_MODEL, D_EXPERT), f32) * D_MODEL ** -0.5,
        'w_down_e': nrm(ks[31], (DEPTH, N_EXPERTS, D_EXPERT, D_MODEL), f32) * D_EXPERT ** -0.5,
    }


def reference(x_prompt, x_sample, cache_k, cache_v, state_conv, page_table, c_prompt, c_sample,
              w_ada, b_ada, g_norm1, g_norm2, w_in, g_qnorm, g_knorm, lam_q1, lam_k1, lam_q2, lam_k2,
              g_subln, w_dw, b_dw, g_ln_conv, b_ln_conv, w_out, w_router_group, b_router_group,
              w_router_expert, b_router_expert, w_gate_e, w_up_e, w_down_e):
    past = page_table.shape[1] * PAGE_SIZE
    pos_prompt = jnp.arange(x_prompt.shape[1])
    pos_sample = past + jnp.arange(x_sample.shape[1])
    xp, xs = x_prompt, x_sample
    kp_l, vp_l, cp_l, ks_l, vs_l, cs_l = [], [], [], [], [], []
    for l in range(DEPTH):
        p = {
            'w_ada': w_ada[l], 'b_ada': b_ada[l], 'g_norm1': g_norm1[l], 'g_norm2': g_norm2[l],
            'w_in': w_in[l], 'g_qnorm': g_qnorm[l], 'g_knorm': g_knorm[l],
            'lam_q1': lam_q1[l], 'lam_k1': lam_k1[l], 'lam_q2': lam_q2[l], 'lam_k2': lam_k2[l],
            'g_subln': g_subln[l], 'w_dw': w_dw[l], 'b_dw': b_dw[l],
            'g_ln_conv': g_ln_conv[l], 'b_ln_conv': b_ln_conv[l], 'w_out': w_out[l],
            'w_router_group': w_router_group[l], 'b_router_group': b_router_group[l],
            'w_router_expert': w_router_expert[l], 'b_router_expert': b_router_expert[l],
            'w_gate_e': w_gate_e[l], 'w_up_e': w_up_e[l], 'w_down_e': w_down_e[l],
        }
        lam_init = lambda_init(l)
        buf0 = jnp.zeros((xp.shape[0], CONV_WIDTH - 1, CONV_CH), xp.dtype)
        xp, kp, vp, cp = decoder_layer(xp, c_prompt, pos_prompt, buf0, prompt_attention, lam_init, p)
        ck, cv = cache_k[l], cache_v[l]
        attend_s = lambda q, k, v, lam, ck=ck, cv=cv: sample_attention(q, k, v, lam, ck, cv, page_table)
        xs, ksm, vsm, csm = decoder_layer(xs, c_sample, pos_sample, state_conv[l], attend_s, lam_init, p)
        kp_l.append(kp); vp_l.append(vp); cp_l.append(cp)
        ks_l.append(ksm); vs_l.append(vsm); cs_l.append(csm)
    k_prompt = jnp.stack(kp_l)
    v_prompt = jnp.stack(vp_l)
    conv_prompt = jnp.stack(cp_l)
    k_sample = jnp.stack(ks_l)
    v_sample = jnp.stack(vs_l)
    conv_sample = jnp.stack(cs_l)
    return (xp, xs, k_prompt, v_prompt, conv_prompt, k_sample, v_sample, conv_sample)
```

```python
import functools
import math

import jax
import jax.numpy as jnp
from jax import lax
from jax.experimental import pallas as pl
from jax.experimental.pallas import tpu as pltpu

F32 = jnp.float32
BF16 = jnp.bfloat16

D_MODEL = 1024
HEAD_DIM = 64
V_DIM = 2 * HEAD_DIM
N_HEADS = (D_MODEL // 2) // V_DIM
D_ATTN = N_HEADS * V_DIM
QK_WIDTH = N_HEADS * 2 * HEAD_DIM
CONV_CH = D_MODEL - D_ATTN
CONV_WIDTH = 31
IN_COLS = 2 * QK_WIDTH + D_ATTN + 2 * CONV_CH
N_GROUPS = 4
EXPERTS_PER_GROUP = 8
N_EXPERTS = N_GROUPS * EXPERTS_PER_GROUP
D_EXPERT = D_MODEL // 4
ROPE_THETA = 10000.0
PAGE_SIZE = 128
EPS = 1e-6
N_MOD = 6
NEG_INF = -1e30

LANES = 128
SUBLANES = 8
VMEM_LIMIT = 48 * 1024 * 1024
HALO = 32
PAGES_PER_STEP = 16
ROUTER_LANES = 128


def _params(*sem):
    return pltpu.CompilerParams(dimension_semantics=sem, vmem_limit_bytes=VMEM_LIMIT)


def _nt_dot(a, b):
    return lax.dot_general(a, b, (((1,), (1,)), ((), ())), preferred_element_type=F32)


def _ada_kernel(c_ref, w_ref, b_ref, o_ref):
    c = c_ref[...]
    a = (c * jax.nn.sigmoid(c)).astype(BF16)
    o_ref[...] = jnp.dot(a, w_ref[...].astype(BF16), preferred_element_type=F32) + b_ref[...]


def _ada(c_all, w_ada, b_ada):
    rows = c_all.shape[0]
    return pl.pallas_call(
        _ada_kernel,
        grid=(N_MOD,),
        in_specs=[
            pl.BlockSpec((rows, D_MODEL), lambda j: (0, 0)),
            pl.BlockSpec((D_MODEL, D_MODEL), lambda j: (0, j)),
            pl.BlockSpec((1, D_MODEL), lambda j: (0, j)),
        ],
        out_specs=pl.BlockSpec((rows, D_MODEL), lambda j: (0, j)),
        out_shape=jax.ShapeDtypeStruct((rows, N_MOD * D_MODEL), F32),
        compiler_params=_params("arbitrary"),
        name="ada",
    )(c_all, w_ada, b_ada.reshape(1, -1))


def _rms_rows(x, g):
    return x * lax.rsqrt(jnp.mean(x * x, axis=-1, keepdims=True) + EPS) * g


def _qk_norm_rope(t, g, seg, cos, sin_signed, first_half):
    ms = jnp.dot((t * t).astype(BF16), seg, preferred_element_type=F32) * (1.0 / HEAD_DIM)
    tn = t * lax.rsqrt(ms + EPS) * g
    out = []
    for i in range(t.shape[1] // LANES):
        xs = tn[:, i * LANES:(i + 1) * LANES]
        rot = jnp.where(first_half, pltpu.roll(xs, LANES - HEAD_DIM // 2, 1), pltpu.roll(xs, HEAD_DIM // 2, 1))
        out.append(xs * cos + rot * sin_signed)
    return jnp.concatenate(out, axis=1)


def _in_proj_kernel(x_ref, sh_ref, sc_ref, g1_ref, w_ref, gq_ref, gk_ref, cos_ref, sin_ref, seg_ref,
                    k_out, v_out, glu_out, q_bf, k_bf, v_bf):
    x = x_ref[...]
    h = _rms_rows(x, g1_ref[...]) * (1.0 + sc_ref[...]) + sh_ref[...]
    proj = jnp.dot(h.astype(BF16), w_ref[...], preferred_element_type=F32)
    cos = cos_ref[...]
    sin = sin_ref[...]
    seg = seg_ref[...]
    lane = lax.broadcasted_iota(jnp.int32, cos.shape, 1)
    first_half = (lane % HEAD_DIM) < (HEAD_DIM // 2)
    q = _qk_norm_rope(proj[:, :QK_WIDTH], gq_ref[...], seg, cos, sin, first_half)
    k = _qk_norm_rope(proj[:, QK_WIDTH:2 * QK_WIDTH], gk_ref[...], seg, cos, sin, first_half)
    v = proj[:, 2 * QK_WIDTH:2 * QK_WIDTH + D_ATTN]
    a = proj[:, 2 * QK_WIDTH + D_ATTN:2 * QK_WIDTH + D_ATTN + CONV_CH]
    gate = proj[:, 2 * QK_WIDTH + D_ATTN + CONV_CH:]
    k_out[...] = k
    v_out[...] = v
    glu_out[...] = a * jax.nn.sigmoid(gate)
    q_bf[...] = (q * (HEAD_DIM ** -0.5)).astype(BF16)
    k_bf[...] = k.astype(BF16)
    v_bf[...] = v.astype(BF16)


def _in_proj(x, sh, sc, g1, w_in_bf, gq, gk, cos, sin, seg, tm):
    rows = x.shape[0]
    per_row = sh.shape[0] != 1
    mod_spec = pl.BlockSpec((tm, D_MODEL), lambda i: (i, 0)) if per_row else pl.BlockSpec((1, D_MODEL), lambda i: (0, 0))
    const = lambda shape: pl.BlockSpec(shape, lambda i: (0, 0))
    row_spec = lambda width: pl.BlockSpec((tm, width), lambda i: (i, 0))
    f32_out = jax.ShapeDtypeStruct((rows, QK_WIDTH), F32)
    bf_out = jax.ShapeDtypeStruct((rows, QK_WIDTH), BF16)
    return pl.pallas_call(
        _in_proj_kernel,
        grid=(rows // tm,),
        in_specs=[row_spec(D_MODEL), mod_spec, mod_spec, const((1, D_MODEL)), const((D_MODEL, IN_COLS)),
                  const((1, QK_WIDTH)), const((1, QK_WIDTH)), row_spec(LANES), row_spec(LANES),
                  const((QK_WIDTH, QK_WIDTH))],
        out_specs=[row_spec(QK_WIDTH)] * 6,
        out_shape=[f32_out, f32_out, f32_out, bf_out, bf_out, bf_out],
        compiler_params=_params("parallel"),
        name="in_proj",
    )(x, sh, sc, g1, w_in_bf, gq, gk, cos, sin, seg)


def _ln_swish(y, g, b):
    mu = jnp.mean(y, axis=-1, keepdims=True)
    d = y - mu
    var = jnp.mean(d * d, axis=-1, keepdims=True)
    z = d * lax.rsqrt(var + EPS) * g + b
    return z * jax.nn.sigmoid(z)


def _conv_prompt_kernel(glu_ref, w_ref, bdw_ref, g_ref, b_ref, o_ref, xpad):
    tm = glu_ref.shape[0]

    @pl.when(pl.program_id(0) == 0)
    def _():
        xpad[0:HALO, :] = jnp.zeros((HALO, CONV_CH), F32)

    xpad[HALO:HALO + tm, :] = glu_ref[...]
    base = HALO - (CONV_WIDTH - 1)
    acc = jnp.zeros((tm, CONV_CH), F32)
    for j in range(CONV_WIDTH):
        acc = acc + xpad[base + j:base + j + tm, :] * w_ref[j:j + 1, :]
    y = acc + bdw_ref[...]
    o_ref[...] = _ln_swish(y, g_ref[...], b_ref[...]).astype(BF16)
    xpad[0:HALO, :] = xpad[tm:tm + HALO, :]


def _conv_prompt(glu, w_dw, b_dw, g_ln, b_ln, tm):
    rows = glu.shape[0]
    const = lambda shape: pl.BlockSpec(shape, lambda i: (0, 0))
    return pl.pallas_call(
        _conv_prompt_kernel,
        grid=(rows // tm,),
        in_specs=[pl.BlockSpec((tm, CONV_CH), lambda i: (i, 0)), const((CONV_WIDTH, CONV_CH)),
                  const((1, CONV_CH)), const((1, CONV_CH)), const((1, CONV_CH))],
        out_specs=pl.BlockSpec((tm, CONV_CH), lambda i: (i, 0)),
        out_shape=jax.ShapeDtypeStruct((rows, CONV_CH), BF16),
        scratch_shapes=[pltpu.VMEM((tm + HALO, CONV_CH), F32)],
        compiler_params=_params("arbitrary"),
        name="conv_prompt",
    )(glu, w_dw, b_dw, g_ln, b_ln)


def _conv_sample_kernel(xp_ref, w_ref, bdw_ref, g_ref, b_ref, o_ref):
    n_new = o_ref.shape[0]
    for i in range(n_new):
        acc = jnp.zeros(o_ref.shape[1:], F32)
        for j in range(CONV_WIDTH):
            acc = acc + xp_ref[i + j] * w_ref[j:j + 1, :]
        o_ref[i] = _ln_swish(acc + bdw_ref[...], g_ref[...], b_ref[...]).astype(BF16)


def _conv_sample(xp_t, w_dw, b_dw, g_ln, b_ln, n_new):
    t, batch, ch = xp_t.shape
    const2 = lambda shape: pl.BlockSpec(shape, lambda i: (0, 0))
    return pl.pallas_call(
        _conv_sample_kernel,
        grid=(1,),
        in_specs=[pl.BlockSpec((t, batch, ch), lambda i: (0, 0, 0)), const2((CONV_WIDTH, ch)),
                  const2((1, ch)), const2((1, ch)), const2((1, ch))],
        out_specs=pl.BlockSpec((n_new, batch, ch), lambda i: (0, 0, 0)),
        out_shape=jax.ShapeDtypeStruct((n_new, batch, ch), BF16),
        compiler_params=_params("arbitrary"),
        name="conv_sample",
    )(xp_t, w_dw, b_dw, g_ln, b_ln)


def _lambda(lq1, lk1, lq2, lk2, lam_init):
    s1 = jnp.sum(lq1[...] * lk1[...], axis=-1, keepdims=True)
    s2 = jnp.sum(lq2[...] * lk2[...], axis=-1, keepdims=True)
    return jnp.exp(s1) - jnp.exp(s2) + lam_init


def _online_step(s, vb, m, l, acc):
    m_new = jnp.maximum(m, jnp.max(s, axis=-1, keepdims=True))
    alpha = jnp.exp(m - m_new)
    e = jnp.exp(s - m_new)
    l = alpha * l + jnp.sum(e, axis=-1, keepdims=True)
    acc = alpha * acc + jnp.dot(e.astype(BF16), vb, preferred_element_type=F32)
    return m_new, l, acc


def _attn_prompt_kernel(q_ref, k_ref, v_ref, lq1, lk1, lq2, lk2, gs_ref, o_ref, *, lam_init):
    tq = q_ref.shape[0]
    qi = pl.program_id(1)
    q = q_ref[...]
    lane = lax.broadcasted_iota(jnp.int32, q.shape, 1)
    zero = jnp.zeros_like(q)
    q2 = jnp.concatenate([jnp.where(lane < HEAD_DIM, q, zero), jnp.where(lane >= HEAD_DIM, q, zero)], axis=0)

    def block(j, carry, masked):
        off = pl.multiple_of(j * tq, tq)
        s = _nt_dot(q2, k_ref[pl.ds(off, tq), :])
        if masked:
            row = lax.broadcasted_iota(jnp.int32, s.shape, 0) % tq
            col = lax.broadcasted_iota(jnp.int32, s.shape, 1)
            s = jnp.where(col <= row, s, NEG_INF)
        return _online_step(s, v_ref[pl.ds(off, tq), :], *carry)

    init = (jnp.full((2 * tq, 1), NEG_INF, F32), jnp.zeros((2 * tq, 1), F32), jnp.zeros((2 * tq, V_DIM), F32))
    carry = lax.fori_loop(0, qi, lambda j, c: block(j, c, False), init)
    _, l, acc = block(qi, carry, True)

    lam = _lambda(lq1, lk1, lq2, lk2, lam_init)
    o = acc[:tq] / l[:tq] - lam * (acc[tq:] / l[tq:])
    o_ref[...] = (_rms_rows(o, gs_ref[...]) * (1.0 - lam_init)).astype(BF16)


def _attn_prompt(q_bf, k_bf, v_bf, lam_p, g_subln, lam_init, tq):
    s = q_bf.shape[0]
    vec = lambda n: pl.BlockSpec((1, n), lambda h, i: (0, 0))
    return pl.pallas_call(
        functools.partial(_attn_prompt_kernel, lam_init=lam_init),
        grid=(N_HEADS, s // tq),
        in_specs=[pl.BlockSpec((tq, V_DIM), lambda h, i: (i, h)),
                  pl.BlockSpec((s, V_DIM), lambda h, i: (0, h)),
                  pl.BlockSpec((s, V_DIM), lambda h, i: (0, h)),
                  vec(HEAD_DIM), vec(HEAD_DIM), vec(HEAD_DIM), vec(HEAD_DIM), vec(V_DIM)],
        out_specs=pl.BlockSpec((tq, V_DIM), lambda h, i: (i, h)),
        out_shape=jax.ShapeDtypeStruct((s, D_ATTN), BF16),
        compiler_params=_params("parallel", "arbitrary"),
        name="attn_prompt",
    )(q_bf, k_bf, v_bf, *lam_p, g_subln)


def _attn_sample_kernel(pt_ref, qbd_ref, kn_ref, vn_ref, *rest, lam_init, n_new):
    del pt_ref
    npg = PAGES_PER_STEP
    kp = rest[:npg]
    vp = rest[npg:2 * npg]
    lq1, lk1, lq2, lk2, gs_ref, o_ref, m_s, l_s, acc_s = rest[2 * npg:]
    j = pl.program_id(1)
    n_rows = qbd_ref.shape[1]
    qpad = n_rows // (2 * N_HEADS)

    @pl.when(j == 0)
    def _():
        m_s[...] = jnp.full(m_s.shape, NEG_INF, F32)
        l_s[...] = jnp.zeros(l_s.shape, F32)
        acc_s[...] = jnp.zeros(acc_s.shape, F32)

    qbd = qbd_ref[0]
    kc = jnp.concatenate([r[0] for r in kp], axis=0).astype(BF16)
    vc = jnp.concatenate([r[0] for r in vp], axis=0).astype(BF16)
    m, l, acc = _online_step(_nt_dot(qbd, kc), vc, m_s[...], l_s[...], acc_s[...])
    m_s[...] = m
    l_s[...] = l
    acc_s[...] = acc

    @pl.when(j == pl.num_programs(1) - 1)
    def _():
        qf = qbd.astype(F32)
        kn = kn_ref[0].astype(BF16).astype(F32)
        vn = vn_ref[0].astype(BF16).astype(F32)
        qpos = lax.broadcasted_iota(jnp.int32, (n_rows, 1), 0) % qpad
        s_new = [jnp.where(t <= qpos, jnp.sum(qf * kn[t:t + 1, :], axis=-1, keepdims=True), NEG_INF)
                 for t in range(n_new)]
        m_f = m
        for s_t in s_new:
            m_f = jnp.maximum(m_f, s_t)
        alpha = jnp.exp(m - m_f)
        l_f = alpha * l
        acc_f = alpha * acc
        for t in range(n_new):
            e_t = jnp.exp(s_new[t] - m_f)
            l_f = l_f + e_t
            acc_f = acc_f + e_t.astype(BF16).astype(F32) * vn[t:t + 1, :]
        lam = _lambda(lq1, lk1, lq2, lk2, lam_init)
        half = n_rows // 2
        od = acc_f[:half] / l_f[:half] - lam * (acc_f[half:] / l_f[half:])
        r2 = lax.broadcasted_iota(jnp.int32, od.shape, 0)
        l2 = lax.broadcasted_iota(jnp.int32, od.shape, 1)
        od = jnp.where(l2 // V_DIM == r2 // qpad, od, 0.0)
        o = od[0:qpad]
        for h in range(1, N_HEADS):
            o = o + od[h * qpad:(h + 1) * qpad]
        gs = gs_ref[...]
        outs = [_rms_rows(o[:, h * V_DIM:(h + 1) * V_DIM], gs) for h in range(N_HEADS)]
        o_ref[0] = (jnp.concatenate(outs, axis=1) * (1.0 - lam_init)).astype(BF16)


def _attn_sample(page_table, qbd, kn_pad, vn_pad, cache_k, cache_v, lam_p, g_subln, lam_init, n_new):
    batch, n_rows, _ = qbd.shape
    qpad = kn_pad.shape[1]
    n_pages = page_table.shape[1]
    steps = n_pages // PAGES_PER_STEP
    pt_flat = page_table.reshape(-1)
    tok = lambda r: pl.BlockSpec((1, r, D_ATTN), lambda b, j, pt: (b, 0, 0))

    def page_spec(p):
        return pl.BlockSpec((1, PAGE_SIZE, D_ATTN),
                            lambda b, j, pt: (pt[b * n_pages + j * PAGES_PER_STEP + p], 0, 0))

    vec = lambda n: pl.BlockSpec((1, n), lambda b, j, pt: (0, 0))
    pages = [page_spec(p) for p in range(PAGES_PER_STEP)]
    grid_spec = pltpu.PrefetchScalarGridSpec(
        num_scalar_prefetch=1,
        grid=(batch, steps),
        in_specs=[tok(n_rows), tok(qpad), tok(qpad)] + pages + pages
                 + [vec(HEAD_DIM), vec(HEAD_DIM), vec(HEAD_DIM), vec(HEAD_DIM), vec(V_DIM)],
        out_specs=tok(qpad),
        scratch_shapes=[pltpu.VMEM((n_rows, 1), F32), pltpu.VMEM((n_rows, 1), F32),
                        pltpu.VMEM((n_rows, D_ATTN), F32)],
    )
    return pl.pallas_call(
        functools.partial(_attn_sample_kernel, lam_init=lam_init, n_new=n_new),
        grid_spec=grid_spec,
        out_shape=jax.ShapeDtypeStruct((batch, qpad, D_ATTN), BF16),
        compiler_params=_params("parallel", "arbitrary"),
        name="attn_sample",
    )(pt_flat, qbd, kn_pad, vn_pad, *([cache_k] * PAGES_PER_STEP), *([cache_v] * PAGES_PER_STEP),
      *lam_p, g_subln)


def _block_diag_queries(q_bf, qpad):
    batch, n_new, _ = q_bf.shape
    qp = jnp.pad(q_bf, ((0, 0), (0, qpad - n_new), (0, 0)))
    qt = jnp.tile(qp, (1, 2 * N_HEADS, 1))
    rep = jnp.arange(2 * N_HEADS * qpad) // qpad
    chunk = (rep % N_HEADS) * 2 + rep // N_HEADS
    keep = (jnp.arange(D_ATTN)[None, :] // HEAD_DIM) == chunk[:, None]
    return jnp.where(keep[None], qt, jnp.zeros_like(qt))


def _lane_min_where(cond, lane, width):
    return jnp.min(jnp.where(cond, lane, width), axis=-1, keepdims=True)


def _out_router_kernel(x_ref, oa_ref, oc_ref, w_ref, gt1_ref, sh2_ref, sc2_ref, g2_ref,
                       wr_hi_ref, wr_lo_ref, br_ref, x1_out, h2_out, gates_out):
    mix = (jnp.dot(oa_ref[...], w_ref[0:D_ATTN, :], preferred_element_type=F32)
           + jnp.dot(oc_ref[...], w_ref[D_ATTN:, :], preferred_element_type=F32))
    x1 = x_ref[...] + gt1_ref[...] * mix
    x1_out[...] = x1
    h2 = _rms_rows(x1, g2_ref[...]) * (1.0 + sc2_ref[...]) + sh2_ref[...]
    h2_out[...] = h2.astype(BF16)

    h_hi = h2.astype(BF16)
    h_lo = (h2 - h_hi.astype(F32)).astype(BF16)
    logits = (jnp.dot(h_hi, wr_hi_ref[...], preferred_element_type=F32)
              + jnp.dot(h_hi, wr_lo_ref[...], preferred_element_type=F32)
              + jnp.dot(h_lo, wr_hi_ref[...], preferred_element_type=F32)) + br_ref[...]
    lane_i = lax.broadcasted_iota(jnp.int32, logits.shape, 1)
    lane = lane_i.astype(F32)
    lg = jnp.where(lane_i >= N_EXPERTS, jnp.where(lane_i < N_EXPERTS + N_GROUPS, logits, NEG_INF), NEG_INF)
    mg = jnp.max(lg, axis=-1, keepdims=True)
    pg_top = 1.0 / jnp.sum(jnp.exp(lg - mg), axis=-1, keepdims=True)
    g_idx = _lane_min_where(lg == mg, lane, float(ROUTER_LANES)) - N_EXPERTS
    lane_group = (lane_i // EXPERTS_PER_GROUP).astype(F32)
    le = jnp.where(lane_i < N_EXPERTS, jnp.where(lane_group == g_idx, logits, NEG_INF), NEG_INF)
    m1 = jnp.max(le, axis=-1, keepdims=True)
    i1 = _lane_min_where(le == m1, lane, float(ROUTER_LANES))
    le2 = jnp.where(lane == i1, NEG_INF, le)
    m2 = jnp.max(le2, axis=-1, keepdims=True)
    i2 = _lane_min_where(le2 == m2, lane, float(ROUTER_LANES))
    r = jnp.exp(m2 - m1)
    w1 = 1.0 / (1.0 + r)
    w2 = r / (1.0 + r)
    gates_out[...] = pg_top * (jnp.where(lane == i1, w1, 0.0) + jnp.where(lane == i2, w2, 0.0))


def _out_router(x, oa, oc, w_out_bf, gt1, sh2, sc2, g2, wr_hi, wr_lo, br, tm):
    rows = x.shape[0]
    per_row = gt1.shape[0] != 1
    mod_spec = pl.BlockSpec((tm, D_MODEL), lambda i: (i, 0)) if per_row else pl.BlockSpec((1, D_MODEL), lambda i: (0, 0))
    const = lambda shape: pl.BlockSpec(shape, lambda i: (0, 0))
    row_spec = lambda width: pl.BlockSpec((tm, width), lambda i: (i, 0))
    return pl.pallas_call(
        _out_router_kernel,
        grid=(rows // tm,),
        in_specs=[row_spec(D_MODEL), row_spec(D_ATTN), row_spec(CONV_CH), const((D_MODEL, D_MODEL)),
                  mod_spec, mod_spec, mod_spec, const((1, D_MODEL)),
                  const((D_MODEL, ROUTER_LANES)), const((D_MODEL, ROUTER_LANES)), const((1, ROUTER_LANES))],
        out_specs=[row_spec(D_MODEL), row_spec(D_MODEL), row_spec(ROUTER_LANES)],
        out_shape=[jax.ShapeDtypeStruct((rows, D_MODEL), F32), jax.ShapeDtypeStruct((rows, D_MODEL), BF16),
                   jax.ShapeDtypeStruct((rows, ROUTER_LANES), F32)],
        compiler_params=_params("parallel"),
        name="out_router",
    )(x, oa, oc, w_out_bf, gt1, sh2, sc2, g2, wr_hi, wr_lo, br)


def _moe_kernel(h_ref, gates_ref, x1_ref, gt2_ref, wg_ref, wu_ref, wd_ref, y_ref, acc):
    e = pl.program_id(1)

    @pl.when(e == 0)
    def _():
        acc[...] = jnp.zeros(acc.shape, F32)

    t = h_ref[...]
    g = jnp.dot(t, wg_ref[0], preferred_element_type=F32)
    u = jnp.dot(t, wu_ref[0], preferred_element_type=F32)
    he = (g * jax.nn.sigmoid(g)) * u
    out = jnp.dot(he.astype(BF16), wd_ref[0], preferred_element_type=F32)
    gates = gates_ref[...]
    lane = lax.broadcasted_iota(jnp.int32, gates.shape, 1)
    ge = jnp.sum(jnp.where(lane == e, gates, 0.0), axis=-1, keepdims=True)
    acc[...] += ge * out

    @pl.when(e == pl.num_programs(1) - 1)
    def _():
        y_ref[...] = x1_ref[...] + gt2_ref[...] * acc[...]


def _moe(h2, gates, x1, gt2, wg_bf, wu_bf, wd_bf, tm):
    rows = h2.shape[0]
    per_row = gt2.shape[0] != 1
    mod_spec = (pl.BlockSpec((tm, D_MODEL), lambda i, e: (i, 0)) if per_row
                else pl.BlockSpec((1, D_MODEL), lambda i, e: (0, 0)))
    row_spec = lambda width: pl.BlockSpec((tm, width), lambda i, e: (i, 0))
    return pl.pallas_call(
        _moe_kernel,
        grid=(rows // tm, N_EXPERTS),
        in_specs=[row_spec(D_MODEL), row_spec(ROUTER_LANES), row_spec(D_MODEL), mod_spec,
                  pl.BlockSpec((1, D_MODEL, D_EXPERT), lambda i, e: (e, 0, 0)),
                  pl.BlockSpec((1, D_MODEL, D_EXPERT), lambda i, e: (e, 0, 0)),
                  pl.BlockSpec((1, D_EXPERT, D_MODEL), lambda i, e: (e, 0, 0))],
        out_specs=row_spec(D_MODEL),
        out_shape=jax.ShapeDtypeStruct((rows, D_MODEL), F32),
        scratch_shapes=[pltpu.VMEM((tm, D_MODEL), F32)],
        compiler_params=_params("parallel", "arbitrary"),
        name="moe",
    )(h2, gates, x1, gt2, wg_bf, wu_bf, wd_bf)


def _rope_tables(pos):
    inv = 1.0 / (ROPE_THETA ** (jnp.arange(0, HEAD_DIM, 2, dtype=F32) / HEAD_DIM))
    ang = pos.astype(F32)[:, None] * inv[None, :]
    cos = jnp.cos(ang)
    sin = jnp.sin(ang)
    reps = LANES // (HEAD_DIM // 2)
    cos_t = jnp.tile(cos, (1, reps))
    sin_t = jnp.tile(jnp.concatenate([-sin, sin], axis=-1), (1, reps // 2))
    return cos_t, sin_t


def _lambda_init(layer):
    return 0.8 - 0.6 * math.exp(-0.3 * layer)


def kernel(x_prompt, x_sample, cache_k, cache_v, state_conv, page_table, c_prompt, c_sample, w_ada, b_ada, g_norm1, g_norm2, w_in, g_qnorm, g_knorm, lam_q1, lam_k1, lam_q2, lam_k2, g_subln, w_dw, b_dw, g_ln_conv, b_ln_conv, w_out, w_router_group, b_router_group, w_router_expert, b_router_expert, w_gate_e, w_up_e, w_down_e):
    depth = w_ada.shape[0]
    assert depth == 1, "single-layer trunk"
    layer = 0
    batch_p, seq, _ = x_prompt.shape
    batch_s, n_new, _ = x_sample.shape
    assert batch_p == 1
    past = page_table.shape[1] * PAGE_SIZE
    n_pool = cache_k.shape[1]
    lam_init = _lambda_init(layer)
    row = lambda a: a[layer].reshape(1, -1)

    n_c = batch_p + batch_s
    n_c_pad = -(-n_c // SUBLANES) * SUBLANES
    c_all = jnp.concatenate([c_prompt, c_sample, jnp.zeros((n_c_pad - n_c, D_MODEL), F32)], axis=0)
    mod = _ada(c_all, w_ada[layer], b_ada[layer])
    mod_p = [mod[0:1, i * D_MODEL:(i + 1) * D_MODEL] for i in range(N_MOD)]
    mod_s = [jnp.repeat(mod[1:1 + batch_s, i * D_MODEL:(i + 1) * D_MODEL], n_new, axis=0) for i in range(N_MOD)]

    w_in_bf = w_in[layer].astype(BF16)
    w_out_bf = w_out[layer].astype(BF16)
    wg_bf = w_gate_e[layer].astype(BF16)
    wu_bf = w_up_e[layer].astype(BF16)
    wd_bf = w_down_e[layer].astype(BF16)
    gq = jnp.tile(g_qnorm[layer], QK_WIDTH // HEAD_DIM).reshape(1, -1)
    gk = jnp.tile(g_knorm[layer], QK_WIDTH // HEAD_DIM).reshape(1, -1)
    seg_id = jnp.arange(QK_WIDTH) // HEAD_DIM
    seg = (seg_id[:, None] == seg_id[None, :]).astype(BF16)
    w_r = jnp.concatenate([w_router_expert[layer], w_router_group[layer],
                           jnp.zeros((D_MODEL, ROUTER_LANES - N_EXPERTS - N_GROUPS), F32)], axis=1)
    wr_hi = w_r.astype(BF16)
    wr_lo = (w_r - wr_hi.astype(F32)).astype(BF16)
    b_r = jnp.concatenate([b_router_expert[layer], b_router_group[layer],
                           jnp.zeros((ROUTER_LANES - N_EXPERTS - N_GROUPS,), F32)]).reshape(1, -1)
    lam_p = [row(lam_q1), row(lam_k1), row(lam_q2), row(lam_k2)]
    gs = row(g_subln)
    conv_p = (w_dw[layer], row(b_dw), row(g_ln_conv), row(b_ln_conv))

    xp = x_prompt.reshape(seq, D_MODEL)
    cos_p, sin_p = _rope_tables(jnp.arange(seq))
    k_p, v_p, glu_p, q_bf, k_bf, v_bf = _in_proj(xp, mod_p[0], mod_p[1], row(g_norm1), w_in_bf, gq, gk,
                                                 cos_p, sin_p, seg, tm=512)
    oc_p = _conv_prompt(glu_p, *conv_p, tm=512)
    oa_p = _attn_prompt(q_bf, k_bf, v_bf, lam_p, gs, lam_init, tq=512)
    x1_p, h2_p, gates_p = _out_router(xp, oa_p, oc_p, w_out_bf, mod_p[2], mod_p[3], mod_p[4], row(g_norm2),
                                      wr_hi, wr_lo, b_r, tm=512)
    y_p = _moe(h2_p, gates_p, x1_p, mod_p[5], wg_bf, wu_bf, wd_bf, tm=1024)

    xs = x_sample.reshape(batch_s * n_new, D_MODEL)
    pos_s = past + (jnp.arange(batch_s * n_new) % n_new)
    cos_s, sin_s = _rope_tables(pos_s)
    rows_s = batch_s * n_new
    k_s, v_s, glu_s, qs_bf, ks_bf, vs_bf = _in_proj(xs, mod_s[0], mod_s[1], row(g_norm1), w_in_bf, gq, gk,
                                                    cos_s, sin_s, seg, tm=rows_s)
    xp_s = jnp.concatenate([state_conv[layer], glu_s.reshape(batch_s, n_new, CONV_CH)], axis=1)
    oc_s = _conv_sample(jnp.transpose(xp_s, (1, 0, 2)), *conv_p, n_new=n_new)
    oc_s = jnp.transpose(oc_s, (1, 0, 2)).reshape(rows_s, CONV_CH)
    tok3 = lambda a: a.reshape(batch_s, n_new, D_ATTN)
    pad_q = lambda a: jnp.pad(tok3(a), ((0, 0), (0, SUBLANES - n_new), (0, 0)))
    oa_s = _attn_sample(page_table, _block_diag_queries(tok3(qs_bf), SUBLANES), pad_q(k_s), pad_q(v_s),
                        cache_k[layer].reshape(n_pool, PAGE_SIZE, D_ATTN),
                        cache_v[layer].reshape(n_pool, PAGE_SIZE, D_ATTN), lam_p, gs, lam_init, n_new)
    oa_s = oa_s[:, :n_new].reshape(rows_s, D_ATTN)
    x1_s, h2_s, gates_s = _out_router(xs, oa_s, oc_s, w_out_bf, mod_s[2], mod_s[3],
                                      mod_s[4], row(g_norm2), wr_hi, wr_lo, b_r, tm=rows_s)
    y_s = _moe(h2_s, gates_s, x1_s, mod_s[5], wg_bf, wu_bf, wd_bf, tm=rows_s)

    hshape = (N_HEADS, 2, HEAD_DIM)
    return (y_p.reshape(batch_p, seq, D_MODEL),
            y_s.reshape(batch_s, n_new, D_MODEL),
            k_p.reshape(depth, batch_p, seq, *hshape),
            v_p.reshape(depth, batch_p, seq, N_HEADS, V_DIM),
            glu_p[seq - (CONV_WIDTH - 1):].reshape(depth, batch_p, CONV_WIDTH - 1, CONV_CH),
            k_s.reshape(depth, batch_s, n_new, *hshape),
            v_s.reshape(depth, batch_s, n_new, N_HEADS, V_DIM),
            xp_s[:, n_new:].reshape(depth, batch_s, CONV_WIDTH - 1, CONV_CH))
```

```python
import functools
import math

import jax
import jax.numpy as jnp
from jax import lax
from jax.experimental import pallas as pl
from jax.experimental.pallas import tpu as pltpu

F32 = jnp.float32
BF16 = jnp.bfloat16

D_MODEL = 1024
HEAD_DIM = 64
V_DIM = 2 * HEAD_DIM
N_HEADS = (D_MODEL // 2) // V_DIM
D_ATTN = N_HEADS * V_DIM
QK_WIDTH = N_HEADS * 2 * HEAD_DIM
CONV_CH = D_MODEL - D_ATTN
CONV_WIDTH = 31
IN_COLS = 2 * QK_WIDTH + D_ATTN + 2 * CONV_CH
N_GROUPS = 4
EXPERTS_PER_GROUP = 8
N_EXPERTS = N_GROUPS * EXPERTS_PER_GROUP
D_EXPERT = D_MODEL // 4
ROPE_THETA = 10000.0
PAGE_SIZE = 128
EPS = 1e-6
N_MOD = 6
NEG_INF = -1e30
QK_SCALE_LOG2 = HEAD_DIM ** -0.5 * math.log2(math.e)

LANES = 128
SUBLANES = 8
VMEM_LIMIT = 48 * 1024 * 1024
HALO = 32
PAGES_PER_STEP = 16
ROUTER_LANES = 128


def _params(*sem):
    return pltpu.CompilerParams(dimension_semantics=sem, vmem_limit_bytes=VMEM_LIMIT)


def _nt_dot(a, b):
    return lax.dot_general(a, b, (((1,), (1,)), ((), ())), preferred_element_type=F32)


def _ada_kernel(c_ref, w_ref, b_ref, o_ref):
    c = c_ref[...]
    a = (c * jax.nn.sigmoid(c)).astype(BF16)
    o_ref[...] = jnp.dot(a, w_ref[...].astype(BF16), preferred_element_type=F32) + b_ref[...]


def _ada(c_all, w_ada, b_ada):
    rows = c_all.shape[0]
    return pl.pallas_call(
        _ada_kernel,
        grid=(N_MOD,),
        in_specs=[
            pl.BlockSpec((rows, D_MODEL), lambda j: (0, 0)),
            pl.BlockSpec((D_MODEL, D_MODEL), lambda j: (0, j)),
            pl.BlockSpec((1, D_MODEL), lambda j: (0, j)),
        ],
        out_specs=pl.BlockSpec((rows, D_MODEL), lambda j: (0, j)),
        out_shape=jax.ShapeDtypeStruct((rows, N_MOD * D_MODEL), F32),
        compiler_params=_params("arbitrary"),
        name="ada",
    )(c_all, w_ada, b_ada.reshape(1, -1))


def _rms_rows(x, g):
    return x * lax.rsqrt(jnp.mean(x * x, axis=-1, keepdims=True) + EPS) * g


def _qk_norm_rope(t, g, seg, cos, sin_signed, first_half):
    ms = jnp.dot((t * t).astype(BF16), seg, preferred_element_type=F32) * (1.0 / HEAD_DIM)
    tn = t * lax.rsqrt(ms + EPS) * g
    out = []
    for i in range(t.shape[1] // LANES):
        xs = tn[:, i * LANES:(i + 1) * LANES]
        rot = jnp.where(first_half, pltpu.roll(xs, LANES - HEAD_DIM // 2, 1), pltpu.roll(xs, HEAD_DIM // 2, 1))
        out.append(xs * cos + rot * sin_signed)
    return jnp.concatenate(out, axis=1)


def _in_proj_kernel(x_ref, sh_ref, sc_ref, g1_ref, w_ref, gq_ref, gk_ref, cos_ref, sin_ref, seg_ref,
                    k_out, v_out, glu_out, q_bf, k_bf, v_bf):
    x = x_ref[...]
    h = _rms_rows(x, g1_ref[...]) * (1.0 + sc_ref[...]) + sh_ref[...]
    proj = jnp.dot(h.astype(BF16), w_ref[...], preferred_element_type=F32)
    cos = cos_ref[...]
    sin = sin_ref[...]
    seg = seg_ref[...]
    lane = lax.broadcasted_iota(jnp.int32, cos.shape, 1)
    first_half = (lane % HEAD_DIM) < (HEAD_DIM // 2)
    q = _qk_norm_rope(proj[:, :QK_WIDTH], gq_ref[...], seg, cos, sin, first_half)
    k = _qk_norm_rope(proj[:, QK_WIDTH:2 * QK_WIDTH], gk_ref[...], seg, cos, sin, first_half)
    v = proj[:, 2 * QK_WIDTH:2 * QK_WIDTH + D_ATTN]
    a = proj[:, 2 * QK_WIDTH + D_ATTN:2 * QK_WIDTH + D_ATTN + CONV_CH]
    gate = proj[:, 2 * QK_WIDTH + D_ATTN + CONV_CH:]
    k_out[...] = k
    v_out[...] = v
    glu_out[...] = a * jax.nn.sigmoid(gate)
    q_bf[...] = (q * QK_SCALE_LOG2).astype(BF16)
    k_bf[...] = k.astype(BF16)
    v_bf[...] = v.astype(BF16)


def _in_proj(x, sh, sc, g1, w_in_bf, gq, gk, cos, sin, seg, tm):
    rows = x.shape[0]
    per_row = sh.shape[0] != 1
    mod_spec = pl.BlockSpec((tm, D_MODEL), lambda i: (i, 0)) if per_row else pl.BlockSpec((1, D_MODEL), lambda i: (0, 0))
    const = lambda shape: pl.BlockSpec(shape, lambda i: (0, 0))
    row_spec = lambda width: pl.BlockSpec((tm, width), lambda i: (i, 0))
    f32_out = jax.ShapeDtypeStruct((rows, QK_WIDTH), F32)
    bf_out = jax.ShapeDtypeStruct((rows, QK_WIDTH), BF16)
    return pl.pallas_call(
        _in_proj_kernel,
        grid=(rows // tm,),
        in_specs=[row_spec(D_MODEL), mod_spec, mod_spec, const((1, D_MODEL)), const((D_MODEL, IN_COLS)),
                  const((1, QK_WIDTH)), const((1, QK_WIDTH)), row_spec(LANES), row_spec(LANES),
                  const((QK_WIDTH, QK_WIDTH))],
        out_specs=[row_spec(QK_WIDTH)] * 6,
        out_shape=[f32_out, f32_out, f32_out, bf_out, bf_out, bf_out],
        compiler_params=_params("parallel"),
        name="in_proj",
    )(x, sh, sc, g1, w_in_bf, gq, gk, cos, sin, seg)


def _ln_swish(y, g, b):
    mu = jnp.mean(y, axis=-1, keepdims=True)
    d = y - mu
    var = jnp.mean(d * d, axis=-1, keepdims=True)
    z = d * lax.rsqrt(var + EPS) * g + b
    return z * jax.nn.sigmoid(z)


def _conv_prompt_kernel(glu_ref, w_ref, bdw_ref, g_ref, b_ref, o_ref, xpad):
    tm = glu_ref.shape[0]

    @pl.when(pl.program_id(0) == 0)
    def _():
        xpad[0:HALO, :] = jnp.zeros((HALO, CONV_CH), F32)

    xpad[HALO:HALO + tm, :] = glu_ref[...]
    base = HALO - (CONV_WIDTH - 1)
    acc = jnp.zeros((tm, CONV_CH), F32)
    for j in range(CONV_WIDTH):
        acc = acc + xpad[base + j:base + j + tm, :] * w_ref[j:j + 1, :]
    y = acc + bdw_ref[...]
    o_ref[...] = _ln_swish(y, g_ref[...], b_ref[...]).astype(BF16)
    xpad[0:HALO, :] = xpad[tm:tm + HALO, :]


def _conv_prompt(glu, w_dw, b_dw, g_ln, b_ln, tm):
    rows = glu.shape[0]
    const = lambda shape: pl.BlockSpec(shape, lambda i: (0, 0))
    return pl.pallas_call(
        _conv_prompt_kernel,
        grid=(rows // tm,),
        in_specs=[pl.BlockSpec((tm, CONV_CH), lambda i: (i, 0)), const((CONV_WIDTH, CONV_CH)),
                  const((1, CONV_CH)), const((1, CONV_CH)), const((1, CONV_CH))],
        out_specs=pl.BlockSpec((tm, CONV_CH), lambda i: (i, 0)),
        out_shape=jax.ShapeDtypeStruct((rows, CONV_CH), BF16),
        scratch_shapes=[pltpu.VMEM((tm + HALO, CONV_CH), F32)],
        compiler_params=_params("arbitrary"),
        name="conv_prompt",
    )(glu, w_dw, b_dw, g_ln, b_ln)


def _conv_sample_kernel(xp_ref, w_ref, bdw_ref, g_ref, b_ref, o_ref):
    n_new = o_ref.shape[0]
    for i in range(n_new):
        acc = jnp.zeros(o_ref.shape[1:], F32)
        for j in range(CONV_WIDTH):
            acc = acc + xp_ref[i + j] * w_ref[j:j + 1, :]
        o_ref[i] = _ln_swish(acc + bdw_ref[...], g_ref[...], b_ref[...]).astype(BF16)


def _conv_sample(xp_t, w_dw, b_dw, g_ln, b_ln, n_new):
    t, batch, ch = xp_t.shape
    const2 = lambda shape: pl.BlockSpec(shape, lambda i: (0, 0))
    return pl.pallas_call(
        _conv_sample_kernel,
        grid=(1,),
        in_specs=[pl.BlockSpec((t, batch, ch), lambda i: (0, 0, 0)), const2((CONV_WIDTH, ch)),
                  const2((1, ch)), const2((1, ch)), const2((1, ch))],
        out_specs=pl.BlockSpec((n_new, batch, ch), lambda i: (0, 0, 0)),
        out_shape=jax.ShapeDtypeStruct((n_new, batch, ch), BF16),
        compiler_params=_params("arbitrary"),
        name="conv_sample",
    )(xp_t, w_dw, b_dw, g_ln, b_ln)


def _lambda(lq1, lk1, lq2, lk2, lam_init):
    s1 = jnp.sum(lq1[...] * lk1[...], axis=-1, keepdims=True)
    s2 = jnp.sum(lq2[...] * lk2[...], axis=-1, keepdims=True)
    return jnp.exp(s1) - jnp.exp(s2) + lam_init


def _online_step(s, vb, m, l, acc):
    m_new = jnp.maximum(m, jnp.max(s, axis=-1, keepdims=True))
    alpha = jnp.exp2(m - m_new)
    e = jnp.exp2(s - m_new)
    l = alpha * l + jnp.sum(e, axis=-1, keepdims=True)
    acc = alpha * acc + jnp.dot(e.astype(BF16), vb, preferred_element_type=F32)
    return m_new, l, acc


def _attn_prompt_kernel(q_ref, k_ref, v_ref, lq1, lk1, lq2, lk2, gs_ref, o_ref, *, lam_init, tk):
    tq = q_ref.shape[0]
    qi = pl.program_id(1)
    q = q_ref[...]
    lane = lax.broadcasted_iota(jnp.int32, q.shape, 1)
    zero = jnp.zeros_like(q)
    q2 = jnp.concatenate([jnp.where(lane < HEAD_DIM, q, zero), jnp.where(lane >= HEAD_DIM, q, zero)], axis=0)

    def block(j, carry, masked):
        off = pl.multiple_of(j * tk, tk)
        s = _nt_dot(q2, k_ref[pl.ds(off, tk), :])
        if masked:
            row = lax.broadcasted_iota(jnp.int32, s.shape, 0) % tq + qi * tq
            col = lax.broadcasted_iota(jnp.int32, s.shape, 1) + off
            s = jnp.where(col <= row, s, NEG_INF)
        return _online_step(s, v_ref[pl.ds(off, tk), :], *carry)

    n_full = (qi * tq) // tk
    init = (jnp.full((2 * tq, 1), NEG_INF, F32), jnp.zeros((2 * tq, 1), F32), jnp.zeros((2 * tq, V_DIM), F32))
    carry = lax.fori_loop(0, n_full, lambda j, c: block(j, c, False), init)
    _, l, acc = block(n_full, carry, True)

    lam = _lambda(lq1, lk1, lq2, lk2, lam_init)
    o = acc[:tq] / l[:tq] - lam * (acc[tq:] / l[tq:])
    o_ref[...] = (_rms_rows(o, gs_ref[...]) * (1.0 - lam_init)).astype(BF16)


def _attn_prompt(q_bf, k_bf, v_bf, lam_p, g_subln, lam_init, tq, tk):
    s = q_bf.shape[0]
    assert tk % tq == 0 and s % tk == 0
    vec = lambda n: pl.BlockSpec((1, n), lambda h, i: (0, 0))
    return pl.pallas_call(
        functools.partial(_attn_prompt_kernel, lam_init=lam_init, tk=tk),
        grid=(N_HEADS, s // tq),
        in_specs=[pl.BlockSpec((tq, V_DIM), lambda h, i: (i, h)),
                  pl.BlockSpec((s, V_DIM), lambda h, i: (0, h)),
                  pl.BlockSpec((s, V_DIM), lambda h, i: (0, h)),
                  vec(HEAD_DIM), vec(HEAD_DIM), vec(HEAD_DIM), vec(HEAD_DIM), vec(V_DIM)],
        out_specs=pl.BlockSpec((tq, V_DIM), lambda h, i: (i, h)),
        out_shape=jax.ShapeDtypeStruct((s, D_ATTN), BF16),
        compiler_params=_params("parallel", "arbitrary"),
        name="attn_prompt",
    )(q_bf, k_bf, v_bf, *lam_p, g_subln)


def _attn_sample_kernel(pt_ref, qbd_ref, kn_ref, vn_ref, *rest, lam_init, n_new):
    del pt_ref
    npg = PAGES_PER_STEP
    kp = rest[:npg]
    vp = rest[npg:2 * npg]
    lq1, lk1, lq2, lk2, gs_ref, o_ref, m_s, l_s, acc_s = rest[2 * npg:]
    j = pl.program_id(1)
    n_rows = qbd_ref.shape[1]
    hrows = n_rows // N_HEADS
    qpad = hrows // 2

    @pl.when(j == 0)
    def _():
        m_s[...] = jnp.full(m_s.shape, NEG_INF, F32)
        l_s[...] = jnp.zeros(l_s.shape, F32)
        acc_s[...] = jnp.zeros(acc_s.shape, F32)

    qbd = qbd_ref[0]
    kc = jnp.concatenate([r[0] for r in kp], axis=1).astype(BF16)
    s = jnp.dot(qbd, kc, preferred_element_type=F32)
    m = m_s[...]
    m_new = jnp.maximum(m, jnp.max(s, axis=-1, keepdims=True))
    alpha = jnp.exp2(m - m_new)
    e = jnp.exp2(s - m_new)
    l = alpha * l_s[...] + jnp.sum(e, axis=-1, keepdims=True)
    e = e.astype(BF16)
    pv = []
    for h in range(N_HEADS):
        vh = jnp.concatenate([r[0, pl.ds(h, PAGE_SIZE, stride=N_HEADS), :] for r in vp], axis=0).astype(BF16)
        pv.append(jnp.dot(e[h * hrows:(h + 1) * hrows], vh, preferred_element_type=F32))
    acc = alpha * acc_s[...] + jnp.concatenate(pv, axis=0)
    m_s[...] = m_new
    l_s[...] = l
    acc_s[...] = acc

    @pl.when(j == pl.num_programs(1) - 1)
    def _():
        qf = qbd.astype(F32)
        kn = kn_ref[0].astype(BF16).astype(F32)
        vn = vn_ref[0].astype(BF16).astype(F32)
        qpos = lax.broadcasted_iota(jnp.int32, (n_rows, 1), 0) % qpad
        s_new = [jnp.where(t <= qpos, jnp.sum(qf * kn[t:t + 1, :], axis=-1, keepdims=True), NEG_INF)
                 for t in range(n_new)]
        m_f = m_new
        for s_t in s_new:
            m_f = jnp.maximum(m_f, s_t)
        a_f = jnp.exp2(m_new - m_f)
        l_f = a_f * l
        acc_f = a_f * acc
        e_new = [jnp.exp2(s_t - m_f) for s_t in s_new]
        for e_t in e_new:
            l_f = l_f + e_t
        lam = _lambda(lq1, lk1, lq2, lk2, lam_init)
        gs = gs_ref[...]
        outs = []
        for h in range(N_HEADS):
            a_h = acc_f[h * hrows:(h + 1) * hrows]
            for t in range(n_new):
                e_t = e_new[t][h * hrows:(h + 1) * hrows].astype(BF16).astype(F32)
                a_h = a_h + e_t * vn[t:t + 1, h * V_DIM:(h + 1) * V_DIM]
            o_h = a_h / l_f[h * hrows:(h + 1) * hrows]
            outs.append(_rms_rows(o_h[:qpad] - lam * o_h[qpad:], gs))
        o_ref[0] = (jnp.concatenate(outs, axis=1) * (1.0 - lam_init)).astype(BF16)


def _attn_sample(page_table, qbd, kn_pad, vn_pad, cache_kt, cache_v, lam_p, g_subln, lam_init, n_new):
    batch, n_rows, _ = qbd.shape
    qpad = kn_pad.shape[1]
    n_pages = page_table.shape[1]
    steps = n_pages // PAGES_PER_STEP
    pt_flat = page_table.reshape(-1)
    tok = lambda r: pl.BlockSpec((1, r, D_ATTN), lambda b, j, pt: (b, 0, 0))

    def page_spec(p, shape):
        return pl.BlockSpec((1,) + shape, lambda b, j, pt: (pt[b * n_pages + j * PAGES_PER_STEP + p], 0, 0))

    vec = lambda n: pl.BlockSpec((1, n), lambda b, j, pt: (0, 0))
    k_pages = [page_spec(p, (D_ATTN, PAGE_SIZE)) for p in range(PAGES_PER_STEP)]
    v_pages = [page_spec(p, (PAGE_SIZE * N_HEADS, V_DIM)) for p in range(PAGES_PER_STEP)]
    grid_spec = pltpu.PrefetchScalarGridSpec(
        num_scalar_prefetch=1,
        grid=(batch, steps),
        in_specs=[tok(n_rows), tok(qpad), tok(qpad)] + k_pages + v_pages
                 + [vec(HEAD_DIM), vec(HEAD_DIM), vec(HEAD_DIM), vec(HEAD_DIM), vec(V_DIM)],
        out_specs=tok(qpad),
        scratch_shapes=[pltpu.VMEM((n_rows, 1), F32), pltpu.VMEM((n_rows, 1), F32),
                        pltpu.VMEM((n_rows, V_DIM), F32)],
    )
    return pl.pallas_call(
        functools.partial(_attn_sample_kernel, lam_init=lam_init, n_new=n_new),
        grid_spec=grid_spec,
        out_shape=jax.ShapeDtypeStruct((batch, qpad, D_ATTN), BF16),
        compiler_params=_params("parallel", "arbitrary"),
        name="attn_sample",
    )(pt_flat, qbd, kn_pad, vn_pad, *([cache_kt] * PAGES_PER_STEP), *([cache_v] * PAGES_PER_STEP),
      *lam_p, g_subln)


def _block_diag_queries(q_bf, qpad):
    batch, n_new, _ = q_bf.shape
    qp = jnp.pad(q_bf, ((0, 0), (0, qpad - n_new), (0, 0)))
    qt = jnp.tile(qp, (1, 2 * N_HEADS, 1))
    chunk = jnp.arange(2 * N_HEADS * qpad) // qpad
    keep = (jnp.arange(D_ATTN)[None, :] // HEAD_DIM) == chunk[:, None]
    return jnp.where(keep[None], qt, jnp.zeros_like(qt))


def _lane_min_where(cond, lane, width):
    return jnp.min(jnp.where(cond, lane, width), axis=-1, keepdims=True)


def _out_router_kernel(x_ref, oa_ref, oc_ref, w_ref, gt1_ref, sh2_ref, sc2_ref, g2_ref,
                       wr_hi_ref, wr_lo_ref, br_ref, x1_out, h2_out, gates_out):
    mix = (jnp.dot(oa_ref[...], w_ref[0:D_ATTN, :], preferred_element_type=F32)
           + jnp.dot(oc_ref[...], w_ref[D_ATTN:, :], preferred_element_type=F32))
    x1 = x_ref[...] + gt1_ref[...] * mix
    x1_out[...] = x1
    h2 = _rms_rows(x1, g2_ref[...]) * (1.0 + sc2_ref[...]) + sh2_ref[...]
    h2_out[...] = h2.astype(BF16)

    h_hi = h2.astype(BF16)
    h_lo = (h2 - h_hi.astype(F32)).astype(BF16)
    logits = (jnp.dot(h_hi, wr_hi_ref[...], preferred_element_type=F32)
              + jnp.dot(h_hi, wr_lo_ref[...], preferred_element_type=F32)
              + jnp.dot(h_lo, wr_hi_ref[...], preferred_element_type=F32)) + br_ref[...]
    lane_i = lax.broadcasted_iota(jnp.int32, logits.shape, 1)
    lane = lane_i.astype(F32)
    lg = jnp.where(lane_i >= N_EXPERTS, jnp.where(lane_i < N_EXPERTS + N_GROUPS, logits, NEG_INF), NEG_INF)
    mg = jnp.max(lg, axis=-1, keepdims=True)
    pg_top = 1.0 / jnp.sum(jnp.exp(lg - mg), axis=-1, keepdims=True)
    g_idx = _lane_min_where(lg == mg, lane, float(ROUTER_LANES)) - N_EXPERTS
    lane_group = (lane_i // EXPERTS_PER_GROUP).astype(F32)
    le = jnp.where(lane_i < N_EXPERTS, jnp.where(lane_group == g_idx, logits, NEG_INF), NEG_INF)
    m1 = jnp.max(le, axis=-1, keepdims=True)
    i1 = _lane_min_where(le == m1, lane, float(ROUTER_LANES))
    le2 = jnp.where(lane == i1, NEG_INF, le)
    m2 = jnp.max(le2, axis=-1, keepdims=True)
    i2 = _lane_min_where(le2 == m2, lane, float(ROUTER_LANES))
    r = jnp.exp(m2 - m1)
    w1 = 1.0 / (1.0 + r)
    w2 = r / (1.0 + r)
    gates_out[...] = pg_top * (jnp.where(lane == i1, w1, 0.0) + jnp.where(lane == i2, w2, 0.0))


def _out_router(x, oa, oc, w_out_bf, gt1, sh2, sc2, g2, wr_hi, wr_lo, br, tm):
    rows = x.shape[0]
    per_row = gt1.shape[0] != 1
    mod_spec = pl.BlockSpec((tm, D_MODEL), lambda i: (i, 0)) if per_row else pl.BlockSpec((1, D_MODEL), lambda i: (0, 0))
    const = lambda shape: pl.BlockSpec(shape, lambda i: (0, 0))
    row_spec = lambda width: pl.BlockSpec((tm, width), lambda i: (i, 0))
    return pl.pallas_call(
        _out_router_kernel,
        grid=(rows // tm,),
        in_specs=[row_spec(D_MODEL), row_spec(D_ATTN), row_spec(CONV_CH), const((D_MODEL, D_MODEL)),
                  mod_spec, mod_spec, mod_spec, const((1, D_MODEL)),
                  const((D_MODEL, ROUTER_LANES)), const((D_MODEL, ROUTER_LANES)), const((1, ROUTER_LANES))],
        out_specs=[row_spec(D_MODEL), row_spec(D_MODEL), row_spec(ROUTER_LANES)],
        out_shape=[jax.ShapeDtypeStruct((rows, D_MODEL), F32), jax.ShapeDtypeStruct((rows, D_MODEL), BF16),
                   jax.ShapeDtypeStruct((rows, ROUTER_LANES), F32)],
        compiler_params=_params("parallel"),
        name="out_router",
    )(x, oa, oc, w_out_bf, gt1, sh2, sc2, g2, wr_hi, wr_lo, br)


def _moe_kernel(h_ref, gates_ref, x1_ref, gt2_ref, wg_ref, wu_ref, wd_ref, y_ref, acc):
    e = pl.program_id(1)

    @pl.when(e == 0)
    def _():
        acc[...] = jnp.zeros(acc.shape, F32)

    t = h_ref[...]
    g = jnp.dot(t, wg_ref[0], preferred_element_type=F32)
    u = jnp.dot(t, wu_ref[0], preferred_element_type=F32)
    he = (g * jax.nn.sigmoid(g)) * u
    out = jnp.dot(he.astype(BF16), wd_ref[0], preferred_element_type=F32)
    gates = gates_ref[...]
    lane = lax.broadcasted_iota(jnp.int32, gates.shape, 1)
    ge = jnp.sum(jnp.where(lane == e, gates, 0.0), axis=-1, keepdims=True)
    acc[...] += ge * out

    @pl.when(e == pl.num_programs(1) - 1)
    def _():
        y_ref[...] = x1_ref[...] + gt2_ref[...] * acc[...]


def _moe(h2, gates, x1, gt2, wg_bf, wu_bf, wd_bf, tm):
    rows = h2.shape[0]
    per_row = gt2.shape[0] != 1
    mod_spec = (pl.BlockSpec((tm, D_MODEL), lambda i, e: (i, 0)) if per_row
                else pl.BlockSpec((1, D_MODEL), lambda i, e: (0, 0)))
    row_spec = lambda width: pl.BlockSpec((tm, width), lambda i, e: (i, 0))
    return pl.pallas_call(
        _moe_kernel,
        grid=(rows // tm, N_EXPERTS),
        in_specs=[row_spec(D_MODEL), row_spec(ROUTER_LANES), row_spec(D_MODEL), mod_spec,
                  pl.BlockSpec((1, D_MODEL, D_EXPERT), lambda i, e: (e, 0, 0)),
                  pl.BlockSpec((1, D_MODEL, D_EXPERT), lambda i, e: (e, 0, 0)),
                  pl.BlockSpec((1, D_EXPERT, D_MODEL), lambda i, e: (e, 0, 0))],
        out_specs=row_spec(D_MODEL),
        out_shape=jax.ShapeDtypeStruct((rows, D_MODEL), F32),
        scratch_shapes=[pltpu.VMEM((tm, D_MODEL), F32)],
        compiler_params=_params("parallel", "arbitrary"),
        name="moe",
    )(h2, gates, x1, gt2, wg_bf, wu_bf, wd_bf)


def _rope_tables(pos):
    inv = 1.0 / (ROPE_THETA ** (jnp.arange(0, HEAD_DIM, 2, dtype=F32) / HEAD_DIM))
    ang = pos.astype(F32)[:, None] * inv[None, :]
    cos = jnp.cos(ang)
    sin = jnp.sin(ang)
    reps = LANES // (HEAD_DIM // 2)
    cos_t = jnp.tile(cos, (1, reps))
    sin_t = jnp.tile(jnp.concatenate([-sin, sin], axis=-1), (1, reps // 2))
    return cos_t, sin_t


def _lambda_init(layer):
    return 0.8 - 0.6 * math.exp(-0.3 * layer)


def kernel(x_prompt, x_sample, cache_k, cache_v, state_conv, page_table, c_prompt, c_sample, w_ada, b_ada, g_norm1, g_norm2, w_in, g_qnorm, g_knorm, lam_q1, lam_k1, lam_q2, lam_k2, g_subln, w_dw, b_dw, g_ln_conv, b_ln_conv, w_out, w_router_group, b_router_group, w_router_expert, b_router_expert, w_gate_e, w_up_e, w_down_e):
    depth = w_ada.shape[0]
    assert depth == 1, "single-layer trunk"
    layer = 0
    batch_p, seq, _ = x_prompt.shape
    batch_s, n_new, _ = x_sample.shape
    assert batch_p == 1
    past = page_table.shape[1] * PAGE_SIZE
    n_pool = cache_k.shape[1]
    lam_init = _lambda_init(layer)
    row = lambda a: a[layer].reshape(1, -1)

    n_c = batch_p + batch_s
    n_c_pad = -(-n_c // SUBLANES) * SUBLANES
    c_all = jnp.concatenate([c_prompt, c_sample, jnp.zeros((n_c_pad - n_c, D_MODEL), F32)], axis=0)
    mod = _ada(c_all, w_ada[layer], b_ada[layer])
    mod_p = [mod[0:1, i * D_MODEL:(i + 1) * D_MODEL] for i in range(N_MOD)]
    mod_s = [jnp.repeat(mod[1:1 + batch_s, i * D_MODEL:(i + 1) * D_MODEL], n_new, axis=0) for i in range(N_MOD)]

    w_in_bf = w_in[layer].astype(BF16)
    w_out_bf = w_out[layer].astype(BF16)
    wg_bf = w_gate_e[layer].astype(BF16)
    wu_bf = w_up_e[layer].astype(BF16)
    wd_bf = w_down_e[layer].astype(BF16)
    gq = jnp.tile(g_qnorm[layer], QK_WIDTH // HEAD_DIM).reshape(1, -1)
    gk = jnp.tile(g_knorm[layer], QK_WIDTH // HEAD_DIM).reshape(1, -1)
    seg_id = jnp.arange(QK_WIDTH) // HEAD_DIM
    seg = (seg_id[:, None] == seg_id[None, :]).astype(BF16)
    w_r = jnp.concatenate([w_router_expert[layer], w_router_group[layer],
                           jnp.zeros((D_MODEL, ROUTER_LANES - N_EXPERTS - N_GROUPS), F32)], axis=1)
    wr_hi = w_r.astype(BF16)
    wr_lo = (w_r - wr_hi.astype(F32)).astype(BF16)
    b_r = jnp.concatenate([b_router_expert[layer], b_router_group[layer],
                           jnp.zeros((ROUTER_LANES - N_EXPERTS - N_GROUPS,), F32)]).reshape(1, -1)
    lam_p = [row(lam_q1), row(lam_k1), row(lam_q2), row(lam_k2)]
    gs = row(g_subln)
    conv_p = (w_dw[layer], row(b_dw), row(g_ln_conv), row(b_ln_conv))

    xp = x_prompt.reshape(seq, D_MODEL)
    cos_p, sin_p = _rope_tables(jnp.arange(seq))
    k_p, v_p, glu_p, q_bf, k_bf, v_bf = _in_proj(xp, mod_p[0], mod_p[1], row(g_norm1), w_in_bf, gq, gk,
                                                 cos_p, sin_p, seg, tm=512)
    oc_p = _conv_prompt(glu_p, *conv_p, tm=512)
    oa_p = _attn_prompt(q_bf, k_bf, v_bf, lam_p, gs, lam_init, tq=512, tk=1024)
    x1_p, h2_p, gates_p = _out_router(xp, oa_p, oc_p, w_out_bf, mod_p[2], mod_p[3], mod_p[4], row(g_norm2),
                                      wr_hi, wr_lo, b_r, tm=512)
    y_p = _moe(h2_p, gates_p, x1_p, mod_p[5], wg_bf, wu_bf, wd_bf, tm=1024)

    xs = x_sample.reshape(batch_s * n_new, D_MODEL)
    pos_s = past + (jnp.arange(batch_s * n_new) % n_new)
    cos_s, sin_s = _rope_tables(pos_s)
    rows_s = batch_s * n_new
    k_s, v_s, glu_s, qs_bf, ks_bf, vs_bf = _in_proj(xs, mod_s[0], mod_s[1], row(g_norm1), w_in_bf, gq, gk,
                                                    cos_s, sin_s, seg, tm=rows_s)
    xp_s = jnp.concatenate([state_conv[layer], glu_s.reshape(batch_s, n_new, CONV_CH)], axis=1)
    oc_s = _conv_sample(jnp.transpose(xp_s, (1, 0, 2)), *conv_p, n_new=n_new)
    oc_s = jnp.transpose(oc_s, (1, 0, 2)).reshape(rows_s, CONV_CH)
    tok3 = lambda a: a.reshape(batch_s, n_new, D_ATTN)
    pad_q = lambda a: jnp.pad(tok3(a), ((0, 0), (0, SUBLANES - n_new), (0, 0)))
    oa_s = _attn_sample(page_table, _block_diag_queries(tok3(qs_bf), SUBLANES), pad_q(k_s), pad_q(v_s),
                        jnp.transpose(cache_k[layer], (0, 2, 3, 4, 1)).reshape(n_pool, D_ATTN, PAGE_SIZE),
                        cache_v[layer].reshape(n_pool, PAGE_SIZE * N_HEADS, V_DIM), lam_p, gs, lam_init, n_new)
    oa_s = oa_s[:, :n_new].reshape(rows_s, D_ATTN)
    x1_s, h2_s, gates_s = _out_router(xs, oa_s, oc_s, w_out_bf, mod_s[2], mod_s[3],
                                      mod_s[4], row(g_norm2), wr_hi, wr_lo, b_r, tm=rows_s)
    y_s = _moe(h2_s, gates_s, x1_s, mod_s[5], wg_bf, wu_bf, wd_bf, tm=rows_s)

    hshape = (N_HEADS, 2, HEAD_DIM)
    return (y_p.reshape(batch_p, seq, D_MODEL),
            y_s.reshape(batch_s, n_new, D_MODEL),
            k_p.reshape(depth, batch_p, seq, *hshape),
            v_p.reshape(depth, batch_p, seq, N_HEADS, V_DIM),
            glu_p[seq - (CONV_WIDTH - 1):].reshape(depth, batch_p, CONV_WIDTH - 1, CONV_CH),
            k_s.reshape(depth, batch_s, n_new, *hshape),
            v_s.reshape(depth, batch_s, n_new, N_HEADS, V_DIM),
            xp_s[:, n_new:].reshape(depth, batch_s, CONV_WIDTH - 1, CONV_CH))
```

```python
import functools
import math

import jax
import jax.numpy as jnp
from jax import lax
from jax.experimental import pallas as pl
from jax.experimental.pallas import tpu as pltpu

F32 = jnp.float32
BF16 = jnp.bfloat16

D_MODEL = 1024
HEAD_DIM = 64
V_DIM = 2 * HEAD_DIM
N_HEADS = (D_MODEL // 2) // V_DIM
D_ATTN = N_HEADS * V_DIM
QK_WIDTH = N_HEADS * 2 * HEAD_DIM
CONV_CH = D_MODEL - D_ATTN
CONV_WIDTH = 31
IN_COLS = 2 * QK_WIDTH + D_ATTN + 2 * CONV_CH
N_GROUPS = 4
EXPERTS_PER_GROUP = 8
N_EXPERTS = N_GROUPS * EXPERTS_PER_GROUP
D_EXPERT = D_MODEL // 4
ROPE_THETA = 10000.0
PAGE_SIZE = 128
EPS = 1e-6
N_MOD = 6
NEG_INF = -1e30
QK_SCALE_LOG2 = HEAD_DIM ** -0.5 * math.log2(math.e)

LANES = 128
SUBLANES = 8
VMEM_LIMIT = 48 * 1024 * 1024
HALO = 32
CONV_CHUNK = 64
PAGES_PER_STEP = 32
ROUTER_LANES = 128
MOE_TB = 1024
MOE_TR = 256
MOE_CH = 512
MOE_ALIGN = 16
MOE_EXPERTS_PER_STEP = 2
MOE_VMEM_LIMIT = 56 * 1024 * 1024


def _params(*sem):
    return pltpu.CompilerParams(dimension_semantics=sem, vmem_limit_bytes=VMEM_LIMIT)


def _nt_dot(a, b):
    return lax.dot_general(a, b, (((1,), (1,)), ((), ())), preferred_element_type=F32)


def _ada_kernel(c_ref, w_ref, b_ref, o_ref):
    c = c_ref[...]
    a = (c * jax.nn.sigmoid(c)).astype(BF16)
    o_ref[...] = jnp.dot(a, w_ref[...].astype(BF16), preferred_element_type=F32) + b_ref[...]


def _ada(c_all, w_ada, b_ada):
    rows = c_all.shape[0]
    return pl.pallas_call(
        _ada_kernel,
        grid=(N_MOD,),
        in_specs=[
            pl.BlockSpec((rows, D_MODEL), lambda j: (0, 0)),
            pl.BlockSpec((D_MODEL, D_MODEL), lambda j: (0, j)),
            pl.BlockSpec((1, D_MODEL), lambda j: (0, j)),
        ],
        out_specs=pl.BlockSpec((rows, D_MODEL), lambda j: (0, j)),
        out_shape=jax.ShapeDtypeStruct((rows, N_MOD * D_MODEL), F32),
        compiler_params=_params("arbitrary"),
        name="ada",
    )(c_all, w_ada, b_ada.reshape(1, -1))


def _rms_rows(x, g):
    return x * lax.rsqrt(jnp.mean(x * x, axis=-1, keepdims=True) + EPS) * g


def _qk_norm_rope(t, g, seg, cos, sin_signed, first_half):
    ms = jnp.dot((t * t).astype(BF16), seg, preferred_element_type=F32) * (1.0 / HEAD_DIM)
    tn = t * lax.rsqrt(ms + EPS) * g
    out = []
    for i in range(t.shape[1] // LANES):
        xs = tn[:, i * LANES:(i + 1) * LANES]
        rot = jnp.where(first_half, pltpu.roll(xs, LANES - HEAD_DIM // 2, 1), pltpu.roll(xs, HEAD_DIM // 2, 1))
        out.append(xs * cos + rot * sin_signed)
    return jnp.concatenate(out, axis=1)


def _in_proj_kernel(x_ref, sh_ref, sc_ref, g1_ref, w_ref, gq_ref, gk_ref, cos_ref, sin_ref, seg_ref,
                    k_out, v_out, glu_out, q_bf, k_bf, v_bf):
    x = x_ref[...]
    h = _rms_rows(x, g1_ref[...]) * (1.0 + sc_ref[...]) + sh_ref[...]
    proj = jnp.dot(h.astype(BF16), w_ref[...], preferred_element_type=F32)
    cos = cos_ref[...]
    sin = sin_ref[...]
    seg = seg_ref[...]
    lane = lax.broadcasted_iota(jnp.int32, cos.shape, 1)
    first_half = (lane % HEAD_DIM) < (HEAD_DIM // 2)
    q = _qk_norm_rope(proj[:, :QK_WIDTH], gq_ref[...], seg, cos, sin, first_half)
    k = _qk_norm_rope(proj[:, QK_WIDTH:2 * QK_WIDTH], gk_ref[...], seg, cos, sin, first_half)
    v = proj[:, 2 * QK_WIDTH:2 * QK_WIDTH + D_ATTN]
    a = proj[:, 2 * QK_WIDTH + D_ATTN:2 * QK_WIDTH + D_ATTN + CONV_CH]
    gate = proj[:, 2 * QK_WIDTH + D_ATTN + CONV_CH:]
    k_out[...] = k
    v_out[...] = v
    glu_out[...] = a * jax.nn.sigmoid(gate)
    q_bf[...] = (q * QK_SCALE_LOG2).astype(BF16)
    k_bf[...] = k.astype(BF16)
    v_bf[...] = v.astype(BF16)


def _in_proj(x, sh, sc, g1, w_in_bf, gq, gk, cos, sin, seg, tm):
    rows = x.shape[0]
    per_row = sh.shape[0] != 1
    mod_spec = pl.BlockSpec((tm, D_MODEL), lambda i: (i, 0)) if per_row else pl.BlockSpec((1, D_MODEL), lambda i: (0, 0))
    const = lambda shape: pl.BlockSpec(shape, lambda i: (0, 0))
    row_spec = lambda width: pl.BlockSpec((tm, width), lambda i: (i, 0))
    f32_out = jax.ShapeDtypeStruct((rows, QK_WIDTH), F32)
    bf_out = jax.ShapeDtypeStruct((rows, QK_WIDTH), BF16)
    return pl.pallas_call(
        _in_proj_kernel,
        grid=(rows // tm,),
        in_specs=[row_spec(D_MODEL), mod_spec, mod_spec, const((1, D_MODEL)), const((D_MODEL, IN_COLS)),
                  const((1, QK_WIDTH)), const((1, QK_WIDTH)), row_spec(LANES), row_spec(LANES),
                  const((QK_WIDTH, QK_WIDTH))],
        out_specs=[row_spec(QK_WIDTH)] * 6,
        out_shape=[f32_out, f32_out, f32_out, bf_out, bf_out, bf_out],
        compiler_params=_params("parallel"),
        name="in_proj",
    )(x, sh, sc, g1, w_in_bf, gq, gk, cos, sin, seg)


def _ln_swish(y, g, b):
    mu = jnp.mean(y, axis=-1, keepdims=True)
    d = y - mu
    var = jnp.mean(d * d, axis=-1, keepdims=True)
    z = d * lax.rsqrt(var + EPS) * g + b
    return z * jax.nn.sigmoid(z)


def _conv_prompt_kernel(glu_ref, w_ref, bdw_ref, g_ref, b_ref, o_ref, xpad, shifted):
    tm = glu_ref.shape[0]

    @pl.when(pl.program_id(0) == 0)
    def _():
        xpad[0:HALO, :] = jnp.zeros((HALO, CONV_CH), F32)

    xpad[HALO:HALO + tm, :] = glu_ref[...]
    base = HALO - (CONV_WIDTH - 1)
    for c in range(tm // CONV_CHUNK):
        r0 = c * CONV_CHUNK
        acc = jnp.zeros((CONV_CHUNK, CONV_CH), F32)
        for phase in range(SUBLANES):
            taps = [j for j in range(CONV_WIDTH) if (base + j) % SUBLANES == phase]
            reach = max(base + j - phase for j in taps) + CONV_CHUNK
            shifted[phase, 0:reach, :] = xpad[r0 + phase:r0 + phase + reach, :]
            for j in taps:
                a = base + j - phase
                acc = acc + shifted[phase, a:a + CONV_CHUNK, :] * w_ref[j:j + 1, :]
        y = acc + bdw_ref[...]
        o_ref[r0:r0 + CONV_CHUNK, :] = _ln_swish(y, g_ref[...], b_ref[...]).astype(BF16)
    xpad[0:HALO, :] = xpad[tm:tm + HALO, :]


def _conv_prompt(glu, w_dw, b_dw, g_ln, b_ln, tm):
    rows = glu.shape[0]
    const = lambda shape: pl.BlockSpec(shape, lambda i: (0, 0))
    return pl.pallas_call(
        _conv_prompt_kernel,
        grid=(rows // tm,),
        in_specs=[pl.BlockSpec((tm, CONV_CH), lambda i: (i, 0)), const((CONV_WIDTH, CONV_CH)),
                  const((1, CONV_CH)), const((1, CONV_CH)), const((1, CONV_CH))],
        out_specs=pl.BlockSpec((tm, CONV_CH), lambda i: (i, 0)),
        out_shape=jax.ShapeDtypeStruct((rows, CONV_CH), BF16),
        scratch_shapes=[pltpu.VMEM((tm + HALO, CONV_CH), F32),
                        pltpu.VMEM((SUBLANES, CONV_CHUNK + HALO, CONV_CH), F32)],
        compiler_params=_params("arbitrary"),
        name="conv_prompt",
    )(glu, w_dw, b_dw, g_ln, b_ln)


def _conv_sample_kernel(xp_ref, w_ref, bdw_ref, g_ref, b_ref, o_ref):
    n_new = o_ref.shape[0]
    for i in range(n_new):
        acc = jnp.zeros(o_ref.shape[1:], F32)
        for j in range(CONV_WIDTH):
            acc = acc + xp_ref[i + j] * w_ref[j:j + 1, :]
        o_ref[i] = _ln_swish(acc + bdw_ref[...], g_ref[...], b_ref[...]).astype(BF16)


def _conv_sample(xp_t, w_dw, b_dw, g_ln, b_ln, n_new):
    t, batch, ch = xp_t.shape
    const2 = lambda shape: pl.BlockSpec(shape, lambda i: (0, 0))
    return pl.pallas_call(
        _conv_sample_kernel,
        grid=(1,),
        in_specs=[pl.BlockSpec((t, batch, ch), lambda i: (0, 0, 0)), const2((CONV_WIDTH, ch)),
                  const2((1, ch)), const2((1, ch)), const2((1, ch))],
        out_specs=pl.BlockSpec((n_new, batch, ch), lambda i: (0, 0, 0)),
        out_shape=jax.ShapeDtypeStruct((n_new, batch, ch), BF16),
        compiler_params=_params("arbitrary"),
        name="conv_sample",
    )(xp_t, w_dw, b_dw, g_ln, b_ln)


def _lambda(lq1, lk1, lq2, lk2, lam_init):
    s1 = jnp.sum(lq1[...] * lk1[...], axis=-1, keepdims=True)
    s2 = jnp.sum(lq2[...] * lk2[...], axis=-1, keepdims=True)
    return jnp.exp(s1) - jnp.exp(s2) + lam_init


def _online_step(s, vb, m, l, acc):
    m_new = jnp.maximum(m, jnp.max(s, axis=-1, keepdims=True))
    alpha = jnp.exp2(m - m_new)
    e = jnp.exp2(s - m_new)
    l = alpha * l + jnp.sum(e, axis=-1, keepdims=True)
    acc = alpha * acc + jnp.dot(e.astype(BF16), vb, preferred_element_type=F32)
    return m_new, l, acc


def _attn_prompt_kernel(q_ref, k_ref, v_ref, lq1, lk1, lq2, lk2, gs_ref, o_ref, s_scr, m_scr, l_scr, acc_scr,
                        *, lam_init, tk):
    tq = q_ref.shape[0]
    qi = pl.program_id(1)
    q = q_ref[...]
    lane = lax.broadcasted_iota(jnp.int32, q.shape, 1)
    zero = jnp.zeros_like(q)
    q2 = jnp.concatenate([jnp.where(lane < HEAD_DIM, q, zero), jnp.where(lane >= HEAD_DIM, q, zero)], axis=0)

    def score(j, slot):
        off = pl.multiple_of(j * tk, tk)
        s_scr[slot] = _nt_dot(q2, k_ref[pl.ds(off, tk), :])

    def consume(j, slot, masked):
        off = pl.multiple_of(j * tk, tk)
        s = s_scr[slot]
        if masked:
            row = lax.broadcasted_iota(jnp.int32, s.shape, 0) % tq + qi * tq
            col = lax.broadcasted_iota(jnp.int32, s.shape, 1) + off
            s = jnp.where(col <= row, s, NEG_INF)
        m, l, acc = _online_step(s, v_ref[pl.ds(off, tk), :], m_scr[...], l_scr[...], acc_scr[...])
        m_scr[...] = m
        l_scr[...] = l
        acc_scr[...] = acc

    m_scr[...] = jnp.full(m_scr.shape, NEG_INF, F32)
    l_scr[...] = jnp.zeros(l_scr.shape, F32)
    acc_scr[...] = jnp.zeros(acc_scr.shape, F32)

    n_full = (qi * tq) // tk
    n_pairs = n_full // 2
    score(0, 0)

    @pl.loop(0, n_pairs)
    def _(p):
        score(2 * p + 1, 1)
        consume(2 * p, 0, False)
        score(2 * p + 2, 0)
        consume(2 * p + 1, 1, False)

    last = 2 * n_pairs

    @pl.when(n_full == last)
    def _():
        consume(last, 0, True)

    @pl.when(n_full != last)
    def _():
        score(last + 1, 1)
        consume(last, 0, False)
        consume(last + 1, 1, True)

    lam = _lambda(lq1, lk1, lq2, lk2, lam_init)
    l = l_scr[...]
    acc = acc_scr[...]
    o = acc[:tq] / l[:tq] - lam * (acc[tq:] / l[tq:])
    o_ref[...] = (_rms_rows(o, gs_ref[...]) * (1.0 - lam_init)).astype(BF16)


def _attn_prompt(q_bf, k_bf, v_bf, lam_p, g_subln, lam_init, tq, tk):
    s = q_bf.shape[0]
    assert tk % tq == 0 and s % tk == 0
    vec = lambda n: pl.BlockSpec((1, n), lambda h, i: (0, 0))
    return pl.pallas_call(
        functools.partial(_attn_prompt_kernel, lam_init=lam_init, tk=tk),
        grid=(N_HEADS, s // tq),
        in_specs=[pl.BlockSpec((tq, V_DIM), lambda h, i: (i, h)),
                  pl.BlockSpec((s, V_DIM), lambda h, i: (0, h)),
                  pl.BlockSpec((s, V_DIM), lambda h, i: (0, h)),
                  vec(HEAD_DIM), vec(HEAD_DIM), vec(HEAD_DIM), vec(HEAD_DIM), vec(V_DIM)],
        out_specs=pl.BlockSpec((tq, V_DIM), lambda h, i: (i, h)),
        out_shape=jax.ShapeDtypeStruct((s, D_ATTN), BF16),
        scratch_shapes=[pltpu.VMEM((2, 2 * tq, tk), F32), pltpu.VMEM((2 * tq, 1), F32),
                        pltpu.VMEM((2 * tq, 1), F32), pltpu.VMEM((2 * tq, V_DIM), F32)],
        compiler_params=_params("parallel", "arbitrary"),
        name="attn_prompt",
    )(q_bf, k_bf, v_bf, *lam_p, g_subln)


def _attn_sample_kernel(pt_ref, qbd_ref, kn_ref, vn_ref, *rest, lam_init, n_new):
    del pt_ref
    npg = PAGES_PER_STEP
    kp = rest[:npg]
    vp = rest[npg:2 * npg]
    lq1, lk1, lq2, lk2, gs_ref, o_ref, m_s, l_s, acc_s = rest[2 * npg:]
    j = pl.program_id(1)
    n_rows = qbd_ref.shape[1]
    hrows = n_rows // N_HEADS
    qpad = hrows // 2

    @pl.when(j == 0)
    def _():
        m_s[...] = jnp.full(m_s.shape, NEG_INF, F32)
        l_s[...] = jnp.zeros(l_s.shape, F32)
        acc_s[...] = jnp.zeros(acc_s.shape, F32)

    qbd = qbd_ref[0]
    kc = jnp.concatenate([r[0] for r in kp], axis=1).astype(BF16)
    s = jnp.dot(qbd, kc, preferred_element_type=F32)
    m = m_s[...]
    m_new = jnp.maximum(m, jnp.max(s, axis=-1, keepdims=True))
    alpha = jnp.exp2(m - m_new)
    e = jnp.exp2(s - m_new)
    l = alpha * l_s[...] + jnp.sum(e, axis=-1, keepdims=True)
    e = e.astype(BF16)
    pv = []
    for h in range(N_HEADS):
        vh = jnp.concatenate([r[0, pl.ds(h, PAGE_SIZE, stride=N_HEADS), :] for r in vp], axis=0).astype(BF16)
        pv.append(jnp.dot(e[h * hrows:(h + 1) * hrows], vh, preferred_element_type=F32))
    acc = alpha * acc_s[...] + jnp.concatenate(pv, axis=0)
    m_s[...] = m_new
    l_s[...] = l
    acc_s[...] = acc

    @pl.when(j == pl.num_programs(1) - 1)
    def _():
        qf = qbd.astype(F32)
        kn = kn_ref[0].astype(BF16).astype(F32)
        vn = vn_ref[0].astype(BF16).astype(F32)
        qpos = lax.broadcasted_iota(jnp.int32, (n_rows, 1), 0) % qpad
        s_new = [jnp.where(t <= qpos, jnp.sum(qf * kn[t:t + 1, :], axis=-1, keepdims=True), NEG_INF)
                 for t in range(n_new)]
        m_f = m_new
        for s_t in s_new:
            m_f = jnp.maximum(m_f, s_t)
        a_f = jnp.exp2(m_new - m_f)
        l_f = a_f * l
        acc_f = a_f * acc
        e_new = [jnp.exp2(s_t - m_f) for s_t in s_new]
        for e_t in e_new:
            l_f = l_f + e_t
        lam = _lambda(lq1, lk1, lq2, lk2, lam_init)
        gs = gs_ref[...]
        outs = []
        for h in range(N_HEADS):
            a_h = acc_f[h * hrows:(h + 1) * hrows]
            for t in range(n_new):
                e_t = e_new[t][h * hrows:(h + 1) * hrows].astype(BF16).astype(F32)
                a_h = a_h + e_t * vn[t:t + 1, h * V_DIM:(h + 1) * V_DIM]
            o_h = a_h / l_f[h * hrows:(h + 1) * hrows]
            outs.append(_rms_rows(o_h[:qpad] - lam * o_h[qpad:], gs))
        o_ref[0] = (jnp.concatenate(outs, axis=1) * (1.0 - lam_init)).astype(BF16)


def _attn_sample(page_table, qbd, kn_pad, vn_pad, cache_kt, cache_v, lam_p, g_subln, lam_init, n_new):
    batch, n_rows, _ = qbd.shape
    qpad = kn_pad.shape[1]
    n_pages = page_table.shape[1]
    steps = n_pages // PAGES_PER_STEP
    pt_flat = page_table.reshape(-1)
    tok = lambda r: pl.BlockSpec((1, r, D_ATTN), lambda b, j, pt: (b, 0, 0))

    def page_spec(p, shape):
        return pl.BlockSpec((1,) + shape, lambda b, j, pt: (pt[b * n_pages + j * PAGES_PER_STEP + p], 0, 0))

    vec = lambda n: pl.BlockSpec((1, n), lambda b, j, pt: (0, 0))
    k_pages = [page_spec(p, (D_ATTN, PAGE_SIZE)) for p in range(PAGES_PER_STEP)]
    v_pages = [page_spec(p, (PAGE_SIZE * N_HEADS, V_DIM)) for p in range(PAGES_PER_STEP)]
    grid_spec = pltpu.PrefetchScalarGridSpec(
        num_scalar_prefetch=1,
        grid=(batch, steps),
        in_specs=[tok(n_rows), tok(qpad), tok(qpad)] + k_pages + v_pages
                 + [vec(HEAD_DIM), vec(HEAD_DIM), vec(HEAD_DIM), vec(HEAD_DIM), vec(V_DIM)],
        out_specs=tok(qpad),
        scratch_shapes=[pltpu.VMEM((n_rows, 1), F32), pltpu.VMEM((n_rows, 1), F32),
                        pltpu.VMEM((n_rows, V_DIM), F32)],
    )
    return pl.pallas_call(
        functools.partial(_attn_sample_kernel, lam_init=lam_init, n_new=n_new),
        grid_spec=grid_spec,
        out_shape=jax.ShapeDtypeStruct((batch, qpad, D_ATTN), BF16),
        compiler_params=_params("parallel", "arbitrary"),
        name="attn_sample",
    )(pt_flat, qbd, kn_pad, vn_pad, *([cache_kt] * PAGES_PER_STEP), *([cache_v] * PAGES_PER_STEP),
      *lam_p, g_subln)


def _block_diag_queries(q_bf, qpad):
    batch, n_new, _ = q_bf.shape
    qp = jnp.pad(q_bf, ((0, 0), (0, qpad - n_new), (0, 0)))
    qt = jnp.tile(qp, (1, 2 * N_HEADS, 1))
    chunk = jnp.arange(2 * N_HEADS * qpad) // qpad
    keep = (jnp.arange(D_ATTN)[None, :] // HEAD_DIM) == chunk[:, None]
    return jnp.where(keep[None], qt, jnp.zeros_like(qt))


def _lane_min_where(cond, lane, width):
    return jnp.min(jnp.where(cond, lane, width), axis=-1, keepdims=True)


def _out_router_kernel(x_ref, oa_ref, oc_ref, w_ref, gt1_ref, sh2_ref, sc2_ref, g2_ref,
                       wr_hi_ref, wr_lo_ref, br_ref, x1_out, h2_out, gates_out, route_out):
    mix = (jnp.dot(oa_ref[...], w_ref[0:D_ATTN, :], preferred_element_type=F32)
           + jnp.dot(oc_ref[...], w_ref[D_ATTN:, :], preferred_element_type=F32))
    x1 = x_ref[...] + gt1_ref[...] * mix
    x1_out[...] = x1
    h2 = _rms_rows(x1, g2_ref[...]) * (1.0 + sc2_ref[...]) + sh2_ref[...]
    h2_out[...] = h2.astype(BF16)

    h_hi = h2.astype(BF16)
    h_lo = (h2 - h_hi.astype(F32)).astype(BF16)
    logits = (jnp.dot(h_hi, wr_hi_ref[...], preferred_element_type=F32)
              + jnp.dot(h_hi, wr_lo_ref[...], preferred_element_type=F32)
              + jnp.dot(h_lo, wr_hi_ref[...], preferred_element_type=F32)) + br_ref[...]
    lane_i = lax.broadcasted_iota(jnp.int32, logits.shape, 1)
    lane = lane_i.astype(F32)
    lg = jnp.where(lane_i >= N_EXPERTS, jnp.where(lane_i < N_EXPERTS + N_GROUPS, logits, NEG_INF), NEG_INF)
    mg = jnp.max(lg, axis=-1, keepdims=True)
    pg_top = 1.0 / jnp.sum(jnp.exp(lg - mg), axis=-1, keepdims=True)
    g_idx = _lane_min_where(lg == mg, lane, float(ROUTER_LANES)) - N_EXPERTS
    lane_group = (lane_i // EXPERTS_PER_GROUP).astype(F32)
    le = jnp.where(lane_i < N_EXPERTS, jnp.where(lane_group == g_idx, logits, NEG_INF), NEG_INF)
    m1 = jnp.max(le, axis=-1, keepdims=True)
    i1 = _lane_min_where(le == m1, lane, float(ROUTER_LANES))
    le2 = jnp.where(lane == i1, NEG_INF, le)
    m2 = jnp.max(le2, axis=-1, keepdims=True)
    i2 = _lane_min_where(le2 == m2, lane, float(ROUTER_LANES))
    r = jnp.exp(m2 - m1)
    w1 = 1.0 / (1.0 + r)
    w2 = r / (1.0 + r)
    g1 = pg_top * w1
    g2 = pg_top * w2
    gates_out[...] = jnp.where(lane == i1, g1, 0.0) + jnp.where(lane == i2, g2, 0.0)
    route_out[...] = jnp.where(lane_i == 0, i1, jnp.where(lane_i == 1, i2, jnp.where(lane_i == 2, g1,
                               jnp.where(lane_i == 3, g2, 0.0))))


def _out_router(x, oa, oc, w_out_bf, gt1, sh2, sc2, g2, wr_hi, wr_lo, br, tm):
    rows = x.shape[0]
    per_row = gt1.shape[0] != 1
    mod_spec = pl.BlockSpec((tm, D_MODEL), lambda i: (i, 0)) if per_row else pl.BlockSpec((1, D_MODEL), lambda i: (0, 0))
    const = lambda shape: pl.BlockSpec(shape, lambda i: (0, 0))
    row_spec = lambda width: pl.BlockSpec((tm, width), lambda i: (i, 0))
    return pl.pallas_call(
        _out_router_kernel,
        grid=(rows // tm,),
        in_specs=[row_spec(D_MODEL), row_spec(D_ATTN), row_spec(CONV_CH), const((D_MODEL, D_MODEL)),
                  mod_spec, mod_spec, mod_spec, const((1, D_MODEL)),
                  const((D_MODEL, ROUTER_LANES)), const((D_MODEL, ROUTER_LANES)), const((1, ROUTER_LANES))],
        out_specs=[row_spec(D_MODEL), row_spec(D_MODEL), row_spec(ROUTER_LANES), row_spec(ROUTER_LANES)],
        out_shape=[jax.ShapeDtypeStruct((rows, D_MODEL), F32), jax.ShapeDtypeStruct((rows, D_MODEL), BF16),
                   jax.ShapeDtypeStruct((rows, ROUTER_LANES), F32), jax.ShapeDtypeStruct((rows, ROUTER_LANES), F32)],
        compiler_params=_params("parallel"),
        name="out_router",
    )(x, oa, oc, w_out_bf, gt1, sh2, sc2, g2, wr_hi, wr_lo, br)


def _moe_kernel(h_ref, gates_ref, x1_ref, gt2_ref, wg_ref, wu_ref, wd_ref, y_ref, acc):
    e = pl.program_id(1)

    @pl.when(e == 0)
    def _():
        acc[...] = jnp.zeros(acc.shape, F32)

    t = h_ref[...]
    g = jnp.dot(t, wg_ref[0], preferred_element_type=F32)
    u = jnp.dot(t, wu_ref[0], preferred_element_type=F32)
    he = (g * jax.nn.sigmoid(g)) * u
    out = jnp.dot(he.astype(BF16), wd_ref[0], preferred_element_type=F32)
    gates = gates_ref[...]
    lane = lax.broadcasted_iota(jnp.int32, gates.shape, 1)
    ge = jnp.sum(jnp.where(lane == e, gates, 0.0), axis=-1, keepdims=True)
    acc[...] += ge * out

    @pl.when(e == pl.num_programs(1) - 1)
    def _():
        y_ref[...] = x1_ref[...] + gt2_ref[...] * acc[...]


def _moe(h2, gates, x1, gt2, wg_bf, wu_bf, wd_bf, tm):
    rows = h2.shape[0]
    per_row = gt2.shape[0] != 1
    mod_spec = (pl.BlockSpec((tm, D_MODEL), lambda i, e: (i, 0)) if per_row
                else pl.BlockSpec((1, D_MODEL), lambda i, e: (0, 0)))
    row_spec = lambda width: pl.BlockSpec((tm, width), lambda i, e: (i, 0))
    return pl.pallas_call(
        _moe_kernel,
        grid=(rows // tm, N_EXPERTS),
        in_specs=[row_spec(D_MODEL), row_spec(ROUTER_LANES), row_spec(D_MODEL), mod_spec,
                  pl.BlockSpec((1, D_MODEL, D_EXPERT), lambda i, e: (e, 0, 0)),
                  pl.BlockSpec((1, D_MODEL, D_EXPERT), lambda i, e: (e, 0, 0)),
                  pl.BlockSpec((1, D_EXPERT, D_MODEL), lambda i, e: (e, 0, 0))],
        out_specs=row_spec(D_MODEL),
        out_shape=jax.ShapeDtypeStruct((rows, D_MODEL), F32),
        scratch_shapes=[pltpu.VMEM((tm, D_MODEL), F32)],
        compiler_params=_params("parallel", "arbitrary"),
        name="moe",
    )(h2, gates, x1, gt2, wg_bf, wu_bf, wd_bf)


def _iota_f32(shape, dim):
    return lax.broadcasted_iota(jnp.int32, shape, dim).astype(F32)


def _one_or_zero(a, b):
    return jnp.where(a, 1.0, jnp.where(b, 1.0, 0.0))


def _moe_sorted_kernel(h_ref, route_ref, route_t_ref, wg_ref, wu_ref, wd_ref, o_ref,
                       xs, out_hi, out_lo, ws, d1c_s, d2c_s, meta):
    s = pl.program_id(1)
    tb = h_ref.shape[0]
    pr = xs.shape[0] - MOE_TR

    @pl.when(s == 0)
    def _():
        rt = route_t_ref[...]
        i1r, i2r, g1r, g2r = rt[0:1], rt[1:2], rt[2:3], rt[3:4]
        eio = _iota_f32((ROUTER_LANES, tb), 0)
        s1t = eio == i1r
        s2t = eio == i2r
        mt = _one_or_zero(s1t, s2t).astype(BF16)
        ranks = []
        for c in range(tb // MOE_TR):
            before = _iota_f32((tb, MOE_TR), 0) < _iota_f32((tb, MOE_TR), 1) + float(c * MOE_TR)
            ranks.append(jnp.dot(mt, jnp.where(before, 1.0, 0.0).astype(BF16), preferred_element_type=F32))
        rank_t = jnp.concatenate(ranks, axis=1)
        cnt_col = jnp.sum(mt.astype(F32), axis=1, keepdims=True)
        seg_col = jnp.floor((cnt_col + (MOE_ALIGN - 1.0)) * (1.0 / MOE_ALIGN))
        lower = jnp.where(_iota_f32((ROUTER_LANES, ROUTER_LANES), 1) < _iota_f32((ROUTER_LANES, ROUTER_LANES), 0),
                          1.0, 0.0).astype(BF16)
        off_col = MOE_ALIGN * jnp.dot(lower, jnp.broadcast_to(seg_col, (ROUTER_LANES, ROUTER_LANES)).astype(BF16),
                                      preferred_element_type=F32)[:, 0:1]
        dest_t = off_col + rank_t
        d1r = jnp.sum(jnp.where(s1t, dest_t, 0.0), axis=0, keepdims=True)
        d2r = jnp.sum(jnp.where(s2t, dest_t, 0.0), axis=0, keepdims=True)

        r = route_ref[...]
        lio = _iota_f32((tb, ROUTER_LANES), 1)
        s1 = lio == r[:, 0:1]
        s2 = lio == r[:, 1:2]
        m = _one_or_zero(s1, s2).astype(BF16)
        ranks = []
        for c in range(tb // MOE_TR):
            before = _iota_f32((MOE_TR, tb), 1) < _iota_f32((MOE_TR, tb), 0) + float(c * MOE_TR)
            ranks.append(jnp.dot(jnp.where(before, 1.0, 0.0).astype(BF16), m, preferred_element_type=F32))
        rank = jnp.concatenate(ranks, axis=0)
        cnt_row = jnp.sum(m.astype(F32), axis=0, keepdims=True)
        seg_row = jnp.floor((cnt_row + (MOE_ALIGN - 1.0)) * (1.0 / MOE_ALIGN))
        upper = jnp.where(_iota_f32((ROUTER_LANES, ROUTER_LANES), 0) < _iota_f32((ROUTER_LANES, ROUTER_LANES), 1),
                          1.0, 0.0).astype(BF16)
        off_row = MOE_ALIGN * jnp.dot(jnp.broadcast_to(seg_row, (SUBLANES, ROUTER_LANES)).astype(BF16), upper,
                                      preferred_element_type=F32)[0:1, :]
        dest = off_row + rank
        d1c_s[...] = jnp.sum(jnp.where(s1, dest, 0.0), axis=-1, keepdims=True)
        d2c_s[...] = jnp.sum(jnp.where(s2, dest, 0.0), axis=-1, keepdims=True)
        off_i = off_row.astype(jnp.int32)
        cnt_i = cnt_row.astype(jnp.int32)
        for e in range(N_EXPERTS):
            meta[e] = off_i[0, e]
            meta[N_EXPERTS + e] = cnt_i[0, e]

        x = h_ref[...]
        for c in range(pr // MOE_CH):
            prow = _iota_f32((MOE_CH, tb), 0) + float(c * MOE_CH)
            h1 = prow == d1r
            h2 = prow == d2r
            xs[c * MOE_CH:(c + 1) * MOE_CH, :] = jnp.dot(_one_or_zero(h1, h2).astype(BF16), x,
                                                       preferred_element_type=F32).astype(BF16)
            ws[c * MOE_CH:(c + 1) * MOE_CH, :] = jnp.sum(jnp.where(h1, g1r, 0.0) + jnp.where(h2, g2r, 0.0),
                                                       axis=-1, keepdims=True)
        xs[pr:pr + MOE_TR, :] = jnp.zeros((MOE_TR, D_MODEL), BF16)
        ws[pr:pr + MOE_TR, :] = jnp.zeros((MOE_TR, 1), F32)
        out_hi[...] = jnp.zeros(out_hi.shape, BF16)
        out_lo[...] = jnp.zeros(out_lo.shape, BF16)

    for k in range(MOE_EXPERTS_PER_STEP):
        e = s * MOE_EXPERTS_PER_STEP + k
        off = meta[e]
        n_tiles = (meta[N_EXPERTS + e] + (MOE_TR - 1)) // MOE_TR

        @pl.loop(0, n_tiles)
        def _(i):
            r0 = pl.multiple_of(off + i * MOE_TR, MOE_ALIGN)
            rows = xs[pl.ds(r0, MOE_TR), :]
            g = jnp.dot(rows, wg_ref[k], preferred_element_type=F32)
            u = jnp.dot(rows, wu_ref[k], preferred_element_type=F32)
            he = (g * jax.nn.sigmoid(g)) * u
            out = jnp.dot(he.astype(BF16), wd_ref[k], preferred_element_type=F32) * ws[pl.ds(r0, MOE_TR), :]
            hi = out.astype(BF16)
            out_hi[pl.ds(r0, MOE_TR), :] = hi
            out_lo[pl.ds(r0, MOE_TR), :] = (out - hi.astype(F32)).astype(BF16)

    @pl.when(s == pl.num_programs(1) - 1)
    def _():
        d1c = d1c_s[...]
        d2c = d2c_s[...]
        acc = jnp.zeros((tb, D_MODEL), F32)
        for c in range(pr // MOE_CH):
            pcol = _iota_f32((tb, MOE_CH), 1) + float(c * MOE_CH)
            pt = _one_or_zero(pcol == d1c, pcol == d2c).astype(BF16)
            acc = (acc + jnp.dot(pt, out_hi[c * MOE_CH:(c + 1) * MOE_CH, :], preferred_element_type=F32)
                   + jnp.dot(pt, out_lo[c * MOE_CH:(c + 1) * MOE_CH, :], preferred_element_type=F32))
        o_ref[...] = acc


def _moe_sorted(h2, route, route_t, wg_bf, wu_bf, wd_bf):
    rows = h2.shape[0]
    tb = MOE_TB
    pr = -(-(2 * tb + N_EXPERTS * (MOE_ALIGN - 1)) // MOE_CH) * MOE_CH
    eps = MOE_EXPERTS_PER_STEP
    return pl.pallas_call(
        _moe_sorted_kernel,
        grid=(rows // tb, N_EXPERTS // eps),
        in_specs=[pl.BlockSpec((tb, D_MODEL), lambda b, s: (b, 0)),
                  pl.BlockSpec((tb, ROUTER_LANES), lambda b, s: (b, 0)),
                  pl.BlockSpec((SUBLANES, tb), lambda b, s: (0, b)),
                  pl.BlockSpec((eps, D_MODEL, D_EXPERT), lambda b, s: (s, 0, 0)),
                  pl.BlockSpec((eps, D_MODEL, D_EXPERT), lambda b, s: (s, 0, 0)),
                  pl.BlockSpec((eps, D_EXPERT, D_MODEL), lambda b, s: (s, 0, 0))],
        out_specs=pl.BlockSpec((tb, D_MODEL), lambda b, s: (b, 0)),
        out_shape=jax.ShapeDtypeStruct((rows, D_MODEL), F32),
        scratch_shapes=[pltpu.VMEM((pr + MOE_TR, D_MODEL), BF16), pltpu.VMEM((pr + MOE_TR, D_MODEL), BF16),
                        pltpu.VMEM((pr + MOE_TR, D_MODEL), BF16), pltpu.VMEM((pr + MOE_TR, 1), F32),
                        pltpu.VMEM((tb, 1), F32), pltpu.VMEM((tb, 1), F32),
                        pltpu.SMEM((2 * N_EXPERTS,), jnp.int32)],
        compiler_params=pltpu.CompilerParams(dimension_semantics=("parallel", "arbitrary"),
                                             vmem_limit_bytes=MOE_VMEM_LIMIT),
        name="moe_sorted",
    )(h2, route, route_t, wg_bf, wu_bf, wd_bf)


def _residual_kernel(x_ref, g_ref, d_ref, o_ref):
    o_ref[...] = x_ref[...] + g_ref[...] * d_ref[...]


def _residual(x1, gt2, delta, tm):
    rows = x1.shape[0]
    row_spec = pl.BlockSpec((tm, D_MODEL), lambda i: (i, 0))
    return pl.pallas_call(
        _residual_kernel,
        grid=(rows // tm,),
        in_specs=[row_spec, pl.BlockSpec((1, D_MODEL), lambda i: (0, 0)), row_spec],
        out_specs=row_spec,
        out_shape=jax.ShapeDtypeStruct((rows, D_MODEL), F32),
        compiler_params=_params("parallel"),
        name="residual",
    )(x1, gt2, delta)


def _rope_tables(pos):
    inv = 1.0 / (ROPE_THETA ** (jnp.arange(0, HEAD_DIM, 2, dtype=F32) / HEAD_DIM))
    ang = pos.astype(F32)[:, None] * inv[None, :]
    cos = jnp.cos(ang)
    sin = jnp.sin(ang)
    reps = LANES // (HEAD_DIM // 2)
    cos_t = jnp.tile(cos, (1, reps))
    sin_t = jnp.tile(jnp.concatenate([-sin, sin], axis=-1), (1, reps // 2))
    return cos_t, sin_t


def _lambda_init(layer):
    return 0.8 - 0.6 * math.exp(-0.3 * layer)


def kernel(x_prompt, x_sample, cache_k, cache_v, state_conv, page_table, c_prompt, c_sample, w_ada, b_ada, g_norm1, g_norm2, w_in, g_qnorm, g_knorm, lam_q1, lam_k1, lam_q2, lam_k2, g_subln, w_dw, b_dw, g_ln_conv, b_ln_conv, w_out, w_router_group, b_router_group, w_router_expert, b_router_expert, w_gate_e, w_up_e, w_down_e):
    depth = w_ada.shape[0]
    assert depth == 1, "single-layer trunk"
    layer = 0
    batch_p, seq, _ = x_prompt.shape
    batch_s, n_new, _ = x_sample.shape
    assert batch_p == 1
    past = page_table.shape[1] * PAGE_SIZE
    n_pool = cache_k.shape[1]
    lam_init = _lambda_init(layer)
    row = lambda a: a[layer].reshape(1, -1)

    n_c = batch_p + batch_s
    n_c_pad = -(-n_c // SUBLANES) * SUBLANES
    c_all = jnp.concatenate([c_prompt, c_sample, jnp.zeros((n_c_pad - n_c, D_MODEL), F32)], axis=0)
    mod = _ada(c_all, w_ada[layer], b_ada[layer])
    mod_p = [mod[0:1, i * D_MODEL:(i + 1) * D_MODEL] for i in range(N_MOD)]
    mod_s = [jnp.repeat(mod[1:1 + batch_s, i * D_MODEL:(i + 1) * D_MODEL], n_new, axis=0) for i in range(N_MOD)]

    w_in_bf = w_in[layer].astype(BF16)
    w_out_bf = w_out[layer].astype(BF16)
    wg_bf = w_gate_e[layer].astype(BF16)
    wu_bf = w_up_e[layer].astype(BF16)
    wd_bf = w_down_e[layer].astype(BF16)
    gq = jnp.tile(g_qnorm[layer], QK_WIDTH // HEAD_DIM).reshape(1, -1)
    gk = jnp.tile(g_knorm[layer], QK_WIDTH // HEAD_DIM).reshape(1, -1)
    seg_id = jnp.arange(QK_WIDTH) // HEAD_DIM
    seg = (seg_id[:, None] == seg_id[None, :]).astype(BF16)
    w_r = jnp.concatenate([w_router_expert[layer], w_router_group[layer],
                           jnp.zeros((D_MODEL, ROUTER_LANES - N_EXPERTS - N_GROUPS), F32)], axis=1)
    wr_hi = w_r.astype(BF16)
    wr_lo = (w_r - wr_hi.astype(F32)).astype(BF16)
    b_r = jnp.concatenate([b_router_expert[layer], b_router_group[layer],
                           jnp.zeros((ROUTER_LANES - N_EXPERTS - N_GROUPS,), F32)]).reshape(1, -1)
    lam_p = [row(lam_q1), row(lam_k1), row(lam_q2), row(lam_k2)]
    gs = row(g_subln)
    conv_p = (w_dw[layer], row(b_dw), row(g_ln_conv), row(b_ln_conv))

    xp = x_prompt.reshape(seq, D_MODEL)
    cos_p, sin_p = _rope_tables(jnp.arange(seq))
    k_p, v_p, glu_p, q_bf, k_bf, v_bf = _in_proj(xp, mod_p[0], mod_p[1], row(g_norm1), w_in_bf, gq, gk,
                                                 cos_p, sin_p, seg, tm=512)
    oc_p = _conv_prompt(glu_p, *conv_p, tm=512)
    oa_p = _attn_prompt(q_bf, k_bf, v_bf, lam_p, gs, lam_init, tq=512, tk=1024)
    x1_p, h2_p, _, route_p = _out_router(xp, oa_p, oc_p, w_out_bf, mod_p[2], mod_p[3], mod_p[4], row(g_norm2),
                                         wr_hi, wr_lo, b_r, tm=512)
    delta_p = _moe_sorted(h2_p, route_p, jnp.transpose(route_p[:, :SUBLANES]), wg_bf, wu_bf, wd_bf)
    y_p = _residual(x1_p, mod_p[5], delta_p, tm=512)

    xs = x_sample.reshape(batch_s * n_new, D_MODEL)
    pos_s = past + (jnp.arange(batch_s * n_new) % n_new)
    cos_s, sin_s = _rope_tables(pos_s)
    rows_s = batch_s * n_new
    k_s, v_s, glu_s, qs_bf, ks_bf, vs_bf = _in_proj(xs, mod_s[0], mod_s[1], row(g_norm1), w_in_bf, gq, gk,
                                                    cos_s, sin_s, seg, tm=rows_s)
    xp_s = jnp.concatenate([state_conv[layer], glu_s.reshape(batch_s, n_new, CONV_CH)], axis=1)
    oc_s = _conv_sample(jnp.transpose(xp_s, (1, 0, 2)), *conv_p, n_new=n_new)
    oc_s = jnp.transpose(oc_s, (1, 0, 2)).reshape(rows_s, CONV_CH)
    tok3 = lambda a: a.reshape(batch_s, n_new, D_ATTN)
    pad_q = lambda a: jnp.pad(tok3(a), ((0, 0), (0, SUBLANES - n_new), (0, 0)))
    oa_s = _attn_sample(page_table, _block_diag_queries(tok3(qs_bf), SUBLANES), pad_q(k_s), pad_q(v_s),
                        jnp.transpose(cache_k[layer], (0, 2, 3, 4, 1)).reshape(n_pool, D_ATTN, PAGE_SIZE),
                        cache_v[layer].reshape(n_pool, PAGE_SIZE * N_HEADS, V_DIM), lam_p, gs, lam_init, n_new)
    oa_s = oa_s[:, :n_new].reshape(rows_s, D_ATTN)
    x1_s, h2_s, gates_s, _ = _out_router(xs, oa_s, oc_s, w_out_bf, mod_s[2], mod_s[3],
                                      mod_s[4], row(g_norm2), wr_hi, wr_lo, b_r, tm=rows_s)
    y_s = _moe(h2_s, gates_s, x1_s, mod_s[5], wg_bf, wu_bf, wd_bf, tm=rows_s)

    hshape = (N_HEADS, 2, HEAD_DIM)
    return (y_p.reshape(batch_p, seq, D_MODEL),
            y_s.reshape(batch_s, n_new, D_MODEL),
            k_p.reshape(depth, batch_p, seq, *hshape),
            v_p.reshape(depth, batch_p, seq, N_HEADS, V_DIM),
            glu_p[seq - (CONV_WIDTH - 1):].reshape(depth, batch_p, CONV_WIDTH - 1, CONV_CH),
            k_s.reshape(depth, batch_s, n_new, *hshape),
            v_s.reshape(depth, batch_s, n_new, N_HEADS, V_DIM),
            xp_s[:, n_new:].reshape(depth, batch_s, CONV_WIDTH - 1, CONV_CH))
```

```python
import functools
import math

import jax
import jax.numpy as jnp
from jax import lax
from jax.experimental import pallas as pl
from jax.experimental.pallas import tpu as pltpu

F32 = jnp.float32
BF16 = jnp.bfloat16

D_MODEL = 1024
HEAD_DIM = 64
V_DIM = 2 * HEAD_DIM
N_HEADS = (D_MODEL // 2) // V_DIM
D_ATTN = N_HEADS * V_DIM
QK_WIDTH = N_HEADS * 2 * HEAD_DIM
CONV_CH = D_MODEL - D_ATTN
CONV_WIDTH = 31
IN_COLS = 2 * QK_WIDTH + D_ATTN + 2 * CONV_CH
N_GROUPS = 4
EXPERTS_PER_GROUP = 8
N_EXPERTS = N_GROUPS * EXPERTS_PER_GROUP
D_EXPERT = D_MODEL // 4
ROPE_THETA = 10000.0
PAGE_SIZE = 128
EPS = 1e-6
N_MOD = 6
NEG_INF = -1e30
QK_SCALE_LOG2 = HEAD_DIM ** -0.5 * math.log2(math.e)

LANES = 128
SUBLANES = 8
VMEM_LIMIT = 48 * 1024 * 1024
HALO = 32
CONV_CHUNK = 64
PAGES_PER_STEP = 32
ROUTER_LANES = 128
MOE_TB = 1024
MOE_TR = 256
MOE_CH = 512
MOE_ALIGN = 16
MOE_EXPERTS_PER_STEP = 2
MOE_VMEM_LIMIT = 56 * 1024 * 1024


def _params(*sem):
    return pltpu.CompilerParams(dimension_semantics=sem, vmem_limit_bytes=VMEM_LIMIT)


def _nt_dot(a, b):
    return lax.dot_general(a, b, (((1,), (1,)), ((), ())), preferred_element_type=F32)


def _ada_kernel(c_ref, w_ref, b_ref, o_ref):
    c = c_ref[...]
    a = (c * jax.nn.sigmoid(c)).astype(BF16)
    o_ref[...] = jnp.dot(a, w_ref[...].astype(BF16), preferred_element_type=F32) + b_ref[...]


def _ada(c_all, w_ada, b_ada):
    rows = c_all.shape[0]
    return pl.pallas_call(
        _ada_kernel,
        grid=(N_MOD,),
        in_specs=[
            pl.BlockSpec((rows, D_MODEL), lambda j: (0, 0)),
            pl.BlockSpec((D_MODEL, D_MODEL), lambda j: (0, j)),
            pl.BlockSpec((1, D_MODEL), lambda j: (0, j)),
        ],
        out_specs=pl.BlockSpec((rows, D_MODEL), lambda j: (0, j)),
        out_shape=jax.ShapeDtypeStruct((rows, N_MOD * D_MODEL), F32),
        compiler_params=_params("arbitrary"),
        name="ada",
    )(c_all, w_ada, b_ada.reshape(1, -1))


def _rms_rows(x, g):
    return x * lax.rsqrt(jnp.mean(x * x, axis=-1, keepdims=True) + EPS) * g


def _qk_norm_rope(t, g, seg, cos, sin_signed, first_half):
    ms = jnp.dot((t * t).astype(BF16), seg, preferred_element_type=F32) * (1.0 / HEAD_DIM)
    tn = t * lax.rsqrt(ms + EPS) * g
    out = []
    for i in range(t.shape[1] // LANES):
        xs = tn[:, i * LANES:(i + 1) * LANES]
        rot = jnp.where(first_half, pltpu.roll(xs, LANES - HEAD_DIM // 2, 1), pltpu.roll(xs, HEAD_DIM // 2, 1))
        out.append(xs * cos + rot * sin_signed)
    return jnp.concatenate(out, axis=1)


def _in_proj_kernel(x_ref, sh_ref, sc_ref, g1_ref, w_ref, gq_ref, gk_ref, cos_ref, sin_ref, seg_ref,
                    k_out, v_out, glu_out, q_bf, k_bf, v_bf):
    x = x_ref[...]
    h = _rms_rows(x, g1_ref[...]) * (1.0 + sc_ref[...]) + sh_ref[...]
    proj = jnp.dot(h.astype(BF16), w_ref[...], preferred_element_type=F32)
    cos = cos_ref[...]
    sin = sin_ref[...]
    seg = seg_ref[...]
    lane = lax.broadcasted_iota(jnp.int32, cos.shape, 1)
    first_half = (lane % HEAD_DIM) < (HEAD_DIM // 2)
    q = _qk_norm_rope(proj[:, :QK_WIDTH], gq_ref[...], seg, cos, sin, first_half)
    k = _qk_norm_rope(proj[:, QK_WIDTH:2 * QK_WIDTH], gk_ref[...], seg, cos, sin, first_half)
    v = proj[:, 2 * QK_WIDTH:2 * QK_WIDTH + D_ATTN]
    a = proj[:, 2 * QK_WIDTH + D_ATTN:2 * QK_WIDTH + D_ATTN + CONV_CH]
    gate = proj[:, 2 * QK_WIDTH + D_ATTN + CONV_CH:]
    k_out[...] = k
    v_out[...] = v
    glu_out[...] = a * jax.nn.sigmoid(gate)
    q_bf[...] = (q * QK_SCALE_LOG2).astype(BF16)
    k_bf[...] = k.astype(BF16)
    v_bf[...] = v.astype(BF16)


def _in_proj(x, sh, sc, g1, w_in_bf, gq, gk, cos, sin, seg, tm):
    rows = x.shape[0]
    per_row = sh.shape[0] != 1
    mod_spec = pl.BlockSpec((tm, D_MODEL), lambda i: (i, 0)) if per_row else pl.BlockSpec((1, D_MODEL), lambda i: (0, 0))
    const = lambda shape: pl.BlockSpec(shape, lambda i: (0, 0))
    row_spec = lambda width: pl.BlockSpec((tm, width), lambda i: (i, 0))
    f32_out = jax.ShapeDtypeStruct((rows, QK_WIDTH), F32)
    bf_out = jax.ShapeDtypeStruct((rows, QK_WIDTH), BF16)
    return pl.pallas_call(
        _in_proj_kernel,
        grid=(rows // tm,),
        in_specs=[row_spec(D_MODEL), mod_spec, mod_spec, const((1, D_MODEL)), const((D_MODEL, IN_COLS)),
                  const((1, QK_WIDTH)), const((1, QK_WIDTH)), row_spec(LANES), row_spec(LANES),
                  const((QK_WIDTH, QK_WIDTH))],
        out_specs=[row_spec(QK_WIDTH)] * 6,
        out_shape=[f32_out, f32_out, f32_out, bf_out, bf_out, bf_out],
        compiler_params=_params("parallel"),
        name="in_proj",
    )(x, sh, sc, g1, w_in_bf, gq, gk, cos, sin, seg)


def _ln_swish(y, g, b):
    mu = jnp.mean(y, axis=-1, keepdims=True)
    d = y - mu
    var = jnp.mean(d * d, axis=-1, keepdims=True)
    z = d * lax.rsqrt(var + EPS) * g + b
    return z * jax.nn.sigmoid(z)


def _conv_prompt_kernel(glu_ref, w_ref, bdw_ref, g_ref, b_ref, o_ref, xpad, shifted):
    tm = glu_ref.shape[0]

    @pl.when(pl.program_id(0) == 0)
    def _():
        xpad[0:HALO, :] = jnp.zeros((HALO, CONV_CH), F32)

    xpad[HALO:HALO + tm, :] = glu_ref[...]
    base = HALO - (CONV_WIDTH - 1)
    for c in range(tm // CONV_CHUNK):
        r0 = c * CONV_CHUNK
        acc = jnp.zeros((CONV_CHUNK, CONV_CH), F32)
        for phase in range(SUBLANES):
            taps = [j for j in range(CONV_WIDTH) if (base + j) % SUBLANES == phase]
            reach = max(base + j - phase for j in taps) + CONV_CHUNK
            shifted[phase, 0:reach, :] = xpad[r0 + phase:r0 + phase + reach, :]
            for j in taps:
                a = base + j - phase
                acc = acc + shifted[phase, a:a + CONV_CHUNK, :] * w_ref[j:j + 1, :]
        y = acc + bdw_ref[...]
        o_ref[r0:r0 + CONV_CHUNK, :] = _ln_swish(y, g_ref[...], b_ref[...]).astype(BF16)
    xpad[0:HALO, :] = xpad[tm:tm + HALO, :]


def _conv_prompt(glu, w_dw, b_dw, g_ln, b_ln, tm):
    rows = glu.shape[0]
    const = lambda shape: pl.BlockSpec(shape, lambda i: (0, 0))
    return pl.pallas_call(
        _conv_prompt_kernel,
        grid=(rows // tm,),
        in_specs=[pl.BlockSpec((tm, CONV_CH), lambda i: (i, 0)), const((CONV_WIDTH, CONV_CH)),
                  const((1, CONV_CH)), const((1, CONV_CH)), const((1, CONV_CH))],
        out_specs=pl.BlockSpec((tm, CONV_CH), lambda i: (i, 0)),
        out_shape=jax.ShapeDtypeStruct((rows, CONV_CH), BF16),
        scratch_shapes=[pltpu.VMEM((tm + HALO, CONV_CH), F32),
                        pltpu.VMEM((SUBLANES, CONV_CHUNK + HALO, CONV_CH), F32)],
        compiler_params=_params("arbitrary"),
        name="conv_prompt",
    )(glu, w_dw, b_dw, g_ln, b_ln)


def _conv_sample_kernel(xp_ref, w_ref, bdw_ref, g_ref, b_ref, o_ref):
    n_new = o_ref.shape[0]
    for i in range(n_new):
        acc = jnp.zeros(o_ref.shape[1:], F32)
        for j in range(CONV_WIDTH):
            acc = acc + xp_ref[i + j] * w_ref[j:j + 1, :]
        o_ref[i] = _ln_swish(acc + bdw_ref[...], g_ref[...], b_ref[...]).astype(BF16)


def _conv_sample(xp_t, w_dw, b_dw, g_ln, b_ln, n_new):
    t, batch, ch = xp_t.shape
    const2 = lambda shape: pl.BlockSpec(shape, lambda i: (0, 0))
    return pl.pallas_call(
        _conv_sample_kernel,
        grid=(1,),
        in_specs=[pl.BlockSpec((t, batch, ch), lambda i: (0, 0, 0)), const2((CONV_WIDTH, ch)),
                  const2((1, ch)), const2((1, ch)), const2((1, ch))],
        out_specs=pl.BlockSpec((n_new, batch, ch), lambda i: (0, 0, 0)),
        out_shape=jax.ShapeDtypeStruct((n_new, batch, ch), BF16),
        compiler_params=_params("arbitrary"),
        name="conv_sample",
    )(xp_t, w_dw, b_dw, g_ln, b_ln)


def _lambda(lq1, lk1, lq2, lk2, lam_init):
    s1 = jnp.sum(lq1[...] * lk1[...], axis=-1, keepdims=True)
    s2 = jnp.sum(lq2[...] * lk2[...], axis=-1, keepdims=True)
    return jnp.exp(s1) - jnp.exp(s2) + lam_init


def _online_step(s, vb, m, l, acc):
    m_new = jnp.maximum(m, jnp.max(s, axis=-1, keepdims=True))
    alpha = jnp.exp2(m - m_new)
    e = jnp.exp2(s - m_new)
    l = alpha * l + jnp.sum(e, axis=-1, keepdims=True)
    acc = alpha * acc + jnp.dot(e.astype(BF16), vb, preferred_element_type=F32)
    return m_new, l, acc


def _attn_prompt_kernel(q_ref, k_ref, v_ref, lq1, lk1, lq2, lk2, gs_ref, o_ref, m_scr, l_scr, acc_scr, lp_scr,
                        kmax_scr, *, lam_init, tk):
    tq = q_ref.shape[0]
    qi = pl.program_id(1)
    q = q_ref[...]
    lane = lax.broadcasted_iota(jnp.int32, q.shape, 1)
    zero = jnp.zeros_like(q)
    q2 = jnp.concatenate([jnp.where(lane < HEAD_DIM, q, zero), jnp.where(lane >= HEAD_DIM, q, zero)], axis=0)

    @pl.when(qi == 0)
    def _():
        def chunk_max(c, carry):
            kb = k_ref[pl.ds(pl.multiple_of(c * tk, tk), tk), :].astype(F32)
            kk = kb * kb
            lane_k = lax.broadcasted_iota(jnp.int32, kk.shape, 1)
            n0 = jnp.sum(jnp.where(lane_k < HEAD_DIM, kk, 0.0), axis=-1, keepdims=True)
            n1 = jnp.sum(jnp.where(lane_k >= HEAD_DIM, kk, 0.0), axis=-1, keepdims=True)
            return (jnp.maximum(carry[0], jnp.max(n0, axis=0, keepdims=True)),
                    jnp.maximum(carry[1], jnp.max(n1, axis=0, keepdims=True)))

        k0, k1 = lax.fori_loop(0, k_ref.shape[0] // tk, chunk_max, (jnp.zeros((1, 1), F32), jnp.zeros((1, 1), F32)))
        kmax_scr[0:1, :] = jnp.broadcast_to(k0, (1, LANES))
        kmax_scr[1:2, :] = jnp.broadcast_to(k1, (1, LANES))

    qf = q2.astype(F32)
    kmax = jnp.concatenate([jnp.broadcast_to(kmax_scr[0:1, 0:1], (tq, 1)),
                            jnp.broadcast_to(kmax_scr[1:2, 0:1], (tq, 1))], axis=0)
    bound = jnp.sqrt(jnp.sum(qf * qf, axis=-1, keepdims=True) * kmax)

    def scores(j, masked):
        off = pl.multiple_of(j * tk, tk)
        s = _nt_dot(q2, k_ref[pl.ds(off, tk), :])
        if masked:
            row = lax.broadcasted_iota(jnp.int32, s.shape, 0) % tq + qi * tq
            col = lax.broadcasted_iota(jnp.int32, s.shape, 1) + off
            s = jnp.where(col <= row, s, NEG_INF)
        return s, v_ref[pl.ds(off, tk), :]

    def fast(j, masked):
        s, vb = scores(j, masked)
        e = jnp.exp2(s - bound)
        part = e[:, 0:LANES]
        for t in range(1, tk // LANES):
            part = part + e[:, t * LANES:(t + 1) * LANES]
        lp_scr[...] += part
        acc_scr[...] += jnp.dot(e.astype(BF16), vb, preferred_element_type=F32)

    def exact(j, masked):
        s, vb = scores(j, masked)
        m, l, acc = _online_step(s, vb, m_scr[...], l_scr[...], acc_scr[...])
        m_scr[...] = m
        l_scr[...] = l
        acc_scr[...] = acc

    def finish(l):
        lam = _lambda(lq1, lk1, lq2, lk2, lam_init)
        acc = acc_scr[...]
        o = acc[:tq] / l[:tq] - lam * (acc[tq:] / l[tq:])
        o_ref[...] = (_rms_rows(o, gs_ref[...]) * (1.0 - lam_init)).astype(BF16)

    n_full = (qi * tq) // tk
    n_pairs = n_full // 2
    last = 2 * n_pairs
    lp_scr[...] = jnp.zeros(lp_scr.shape, F32)
    acc_scr[...] = jnp.zeros(acc_scr.shape, F32)

    @pl.loop(0, n_pairs)
    def _(p):
        fast(2 * p, False)
        fast(2 * p + 1, False)

    @pl.when(n_full == last)
    def _():
        fast(last, True)

    @pl.when(n_full != last)
    def _():
        fast(last, False)
        fast(last + 1, True)

    l_fast = jnp.sum(lp_scr[...], axis=-1, keepdims=True)
    trusted = jnp.logical_and(jnp.min(l_fast) > 2.0 ** -60, jnp.max(l_fast) < 2.0 ** 100)

    @pl.when(trusted)
    def _():
        finish(l_fast)

    @pl.when(jnp.logical_not(trusted))
    def _():
        m_scr[...] = jnp.full(m_scr.shape, NEG_INF, F32)
        l_scr[...] = jnp.zeros(l_scr.shape, F32)
        acc_scr[...] = jnp.zeros(acc_scr.shape, F32)

        @pl.loop(0, n_full)
        def _(j):
            exact(j, False)

        exact(n_full, True)
        finish(l_scr[...])


def _attn_prompt(q_bf, k_bf, v_bf, lam_p, g_subln, lam_init, tq, tk):
    s = q_bf.shape[0]
    assert tk % tq == 0 and s % tk == 0
    vec = lambda n: pl.BlockSpec((1, n), lambda h, i: (0, 0))
    return pl.pallas_call(
        functools.partial(_attn_prompt_kernel, lam_init=lam_init, tk=tk),
        grid=(N_HEADS, s // tq),
        in_specs=[pl.BlockSpec((tq, V_DIM), lambda h, i: (i, h)),
                  pl.BlockSpec((s, V_DIM), lambda h, i: (0, h)),
                  pl.BlockSpec((s, V_DIM), lambda h, i: (0, h)),
                  vec(HEAD_DIM), vec(HEAD_DIM), vec(HEAD_DIM), vec(HEAD_DIM), vec(V_DIM)],
        out_specs=pl.BlockSpec((tq, V_DIM), lambda h, i: (i, h)),
        out_shape=jax.ShapeDtypeStruct((s, D_ATTN), BF16),
        scratch_shapes=[pltpu.VMEM((2 * tq, 1), F32), pltpu.VMEM((2 * tq, 1), F32),
                        pltpu.VMEM((2 * tq, V_DIM), F32), pltpu.VMEM((2 * tq, LANES), F32),
                        pltpu.VMEM((SUBLANES, LANES), F32)],
        compiler_params=_params("arbitrary", "arbitrary"),
        name="attn_prompt",
    )(q_bf, k_bf, v_bf, *lam_p, g_subln)


def _attn_sample_kernel(pt_ref, qbd_ref, kn_ref, vn_ref, *rest, lam_init, n_new):
    del pt_ref
    npg = PAGES_PER_STEP
    kp = rest[:npg]
    vp = rest[npg:2 * npg]
    lq1, lk1, lq2, lk2, gs_ref, o_ref, m_s, l_s, acc_s = rest[2 * npg:]
    j = pl.program_id(1)
    n_rows = qbd_ref.shape[1]
    hrows = n_rows // N_HEADS
    qpad = hrows // 2

    @pl.when(j == 0)
    def _():
        m_s[...] = jnp.full(m_s.shape, NEG_INF, F32)
        l_s[...] = jnp.zeros(l_s.shape, F32)
        acc_s[...] = jnp.zeros(acc_s.shape, F32)

    qbd = qbd_ref[0]
    kc = jnp.concatenate([r[0] for r in kp], axis=1).astype(BF16)
    s = jnp.dot(qbd, kc, preferred_element_type=F32)
    m = m_s[...]
    m_new = jnp.maximum(m, jnp.max(s, axis=-1, keepdims=True))
    alpha = jnp.exp2(m - m_new)
    e = jnp.exp2(s - m_new)
    l = alpha * l_s[...] + jnp.sum(e, axis=-1, keepdims=True)
    e = e.astype(BF16)
    pv = []
    for h in range(N_HEADS):
        vh = jnp.concatenate([r[0, pl.ds(h, PAGE_SIZE, stride=N_HEADS), :] for r in vp], axis=0).astype(BF16)
        pv.append(jnp.dot(e[h * hrows:(h + 1) * hrows], vh, preferred_element_type=F32))
    acc = alpha * acc_s[...] + jnp.concatenate(pv, axis=0)
    m_s[...] = m_new
    l_s[...] = l
    acc_s[...] = acc

    @pl.when(j == pl.num_programs(1) - 1)
    def _():
        qf = qbd.astype(F32)
        kn = kn_ref[0].astype(BF16).astype(F32)
        vn = vn_ref[0].astype(BF16).astype(F32)
        qpos = lax.broadcasted_iota(jnp.int32, (n_rows, 1), 0) % qpad
        s_new = [jnp.where(t <= qpos, jnp.sum(qf * kn[t:t + 1, :], axis=-1, keepdims=True), NEG_INF)
                 for t in range(n_new)]
        m_f = m_new
        for s_t in s_new:
            m_f = jnp.maximum(m_f, s_t)
        a_f = jnp.exp2(m_new - m_f)
        l_f = a_f * l
        acc_f = a_f * acc
        e_new = [jnp.exp2(s_t - m_f) for s_t in s_new]
        for e_t in e_new:
            l_f = l_f + e_t
        lam = _lambda(lq1, lk1, lq2, lk2, lam_init)
        gs = gs_ref[...]
        outs = []
        for h in range(N_HEADS):
            a_h = acc_f[h * hrows:(h + 1) * hrows]
            for t in range(n_new):
                e_t = e_new[t][h * hrows:(h + 1) * hrows].astype(BF16).astype(F32)
                a_h = a_h + e_t * vn[t:t + 1, h * V_DIM:(h + 1) * V_DIM]
            o_h = a_h / l_f[h * hrows:(h + 1) * hrows]
            outs.append(_rms_rows(o_h[:qpad] - lam * o_h[qpad:], gs))
        o_ref[0] = (jnp.concatenate(outs, axis=1) * (1.0 - lam_init)).astype(BF16)


def _attn_sample(page_table, qbd, kn_pad, vn_pad, cache_kt, cache_v, lam_p, g_subln, lam_init, n_new):
    batch, n_rows, _ = qbd.shape
    qpad = kn_pad.shape[1]
    n_pages = page_table.shape[1]
    steps = n_pages // PAGES_PER_STEP
    pt_flat = page_table.reshape(-1)
    tok = lambda r: pl.BlockSpec((1, r, D_ATTN), lambda b, j, pt: (b, 0, 0))

    def page_spec(p, shape):
        return pl.BlockSpec((1,) + shape, lambda b, j, pt: (pt[b * n_pages + j * PAGES_PER_STEP + p], 0, 0))

    vec = lambda n: pl.BlockSpec((1, n), lambda b, j, pt: (0, 0))
    k_pages = [page_spec(p, (D_ATTN, PAGE_SIZE)) for p in range(PAGES_PER_STEP)]
    v_pages = [page_spec(p, (PAGE_SIZE * N_HEADS, V_DIM)) for p in range(PAGES_PER_STEP)]
    grid_spec = pltpu.PrefetchScalarGridSpec(
        num_scalar_prefetch=1,
        grid=(batch, steps),
        in_specs=[tok(n_rows), tok(qpad), tok(qpad)] + k_pages + v_pages
                 + [vec(HEAD_DIM), vec(HEAD_DIM), vec(HEAD_DIM), vec(HEAD_DIM), vec(V_DIM)],
        out_specs=tok(qpad),
        scratch_shapes=[pltpu.VMEM((n_rows, 1), F32), pltpu.VMEM((n_rows, 1), F32),
                        pltpu.VMEM((n_rows, V_DIM), F32)],
    )
    return pl.pallas_call(
        functools.partial(_attn_sample_kernel, lam_init=lam_init, n_new=n_new),
        grid_spec=grid_spec,
        out_shape=jax.ShapeDtypeStruct((batch, qpad, D_ATTN), BF16),
        compiler_params=_params("parallel", "arbitrary"),
        name="attn_sample",
    )(pt_flat, qbd, kn_pad, vn_pad, *([cache_kt] * PAGES_PER_STEP), *([cache_v] * PAGES_PER_STEP),
      *lam_p, g_subln)


def _block_diag_queries(q_bf, qpad):
    batch, n_new, _ = q_bf.shape
    qp = jnp.pad(q_bf, ((0, 0), (0, qpad - n_new), (0, 0)))
    qt = jnp.tile(qp, (1, 2 * N_HEADS, 1))
    chunk = jnp.arange(2 * N_HEADS * qpad) // qpad
    keep = (jnp.arange(D_ATTN)[None, :] // HEAD_DIM) == chunk[:, None]
    return jnp.where(keep[None], qt, jnp.zeros_like(qt))


def _lane_min_where(cond, lane, width):
    return jnp.min(jnp.where(cond, lane, width), axis=-1, keepdims=True)


def _out_router_kernel(x_ref, oa_ref, oc_ref, w_ref, gt1_ref, sh2_ref, sc2_ref, g2_ref,
                       wr_hi_ref, wr_lo_ref, br_ref, x1_out, h2_out, gates_out, route_out):
    mix = (jnp.dot(oa_ref[...], w_ref[0:D_ATTN, :], preferred_element_type=F32)
           + jnp.dot(oc_ref[...], w_ref[D_ATTN:, :], preferred_element_type=F32))
    x1 = x_ref[...] + gt1_ref[...] * mix
    x1_out[...] = x1
    h2 = _rms_rows(x1, g2_ref[...]) * (1.0 + sc2_ref[...]) + sh2_ref[...]
    h2_out[...] = h2.astype(BF16)

    h_hi = h2.astype(BF16)
    h_lo = (h2 - h_hi.astype(F32)).astype(BF16)
    logits = (jnp.dot(h_hi, wr_hi_ref[...], preferred_element_type=F32)
              + jnp.dot(h_hi, wr_lo_ref[...], preferred_element_type=F32)
              + jnp.dot(h_lo, wr_hi_ref[...], preferred_element_type=F32)) + br_ref[...]
    lane_i = lax.broadcasted_iota(jnp.int32, logits.shape, 1)
    lane = lane_i.astype(F32)
    lg = jnp.where(lane_i >= N_EXPERTS, jnp.where(lane_i < N_EXPERTS + N_GROUPS, logits, NEG_INF), NEG_INF)
    mg = jnp.max(lg, axis=-1, keepdims=True)
    pg_top = 1.0 / jnp.sum(jnp.exp(lg - mg), axis=-1, keepdims=True)
    g_idx = _lane_min_where(lg == mg, lane, float(ROUTER_LANES)) - N_EXPERTS
    lane_group = (lane_i // EXPERTS_PER_GROUP).astype(F32)
    le = jnp.where(lane_i < N_EXPERTS, jnp.where(lane_group == g_idx, logits, NEG_INF), NEG_INF)
    m1 = jnp.max(le, axis=-1, keepdims=True)
    i1 = _lane_min_where(le == m1, lane, float(ROUTER_LANES))
    le2 = jnp.where(lane == i1, NEG_INF, le)
    m2 = jnp.max(le2, axis=-1, keepdims=True)
    i2 = _lane_min_where(le2 == m2, lane, float(ROUTER_LANES))
    r = jnp.exp(m2 - m1)
    w1 = 1.0 / (1.0 + r)
    w2 = r / (1.0 + r)
    g1 = pg_top * w1
    g2 = pg_top * w2
    gates_out[...] = jnp.where(lane == i1, g1, 0.0) + jnp.where(lane == i2, g2, 0.0)
    route_out[...] = jnp.where(lane_i == 0, i1, jnp.where(lane_i == 1, i2, jnp.where(lane_i == 2, g1,
                               jnp.where(lane_i == 3, g2, 0.0))))


def _out_router(x, oa, oc, w_out_bf, gt1, sh2, sc2, g2, wr_hi, wr_lo, br, tm):
    rows = x.shape[0]
    per_row = gt1.shape[0] != 1
    mod_spec = pl.BlockSpec((tm, D_MODEL), lambda i: (i, 0)) if per_row else pl.BlockSpec((1, D_MODEL), lambda i: (0, 0))
    const = lambda shape: pl.BlockSpec(shape, lambda i: (0, 0))
    row_spec = lambda width: pl.BlockSpec((tm, width), lambda i: (i, 0))
    return pl.pallas_call(
        _out_router_kernel,
        grid=(rows // tm,),
        in_specs=[row_spec(D_MODEL), row_spec(D_ATTN), row_spec(CONV_CH), const((D_MODEL, D_MODEL)),
                  mod_spec, mod_spec, mod_spec, const((1, D_MODEL)),
                  const((D_MODEL, ROUTER_LANES)), const((D_MODEL, ROUTER_LANES)), const((1, ROUTER_LANES))],
        out_specs=[row_spec(D_MODEL), row_spec(D_MODEL), row_spec(ROUTER_LANES), row_spec(ROUTER_LANES)],
        out_shape=[jax.ShapeDtypeStruct((rows, D_MODEL), F32), jax.ShapeDtypeStruct((rows, D_MODEL), BF16),
                   jax.ShapeDtypeStruct((rows, ROUTER_LANES), F32), jax.ShapeDtypeStruct((rows, ROUTER_LANES), F32)],
        compiler_params=_params("parallel"),
        name="out_router",
    )(x, oa, oc, w_out_bf, gt1, sh2, sc2, g2, wr_hi, wr_lo, br)


def _moe_kernel(h_ref, gates_ref, x1_ref, gt2_ref, wg_ref, wu_ref, wd_ref, y_ref, acc):
    e = pl.program_id(1)

    @pl.when(e == 0)
    def _():
        acc[...] = jnp.zeros(acc.shape, F32)

    t = h_ref[...]
    g = jnp.dot(t, wg_ref[0], preferred_element_type=F32)
    u = jnp.dot(t, wu_ref[0], preferred_element_type=F32)
    he = (g * jax.nn.sigmoid(g)) * u
    out = jnp.dot(he.astype(BF16), wd_ref[0], preferred_element_type=F32)
    gates = gates_ref[...]
    lane = lax.broadcasted_iota(jnp.int32, gates.shape, 1)
    ge = jnp.sum(jnp.where(lane == e, gates, 0.0), axis=-1, keepdims=True)
    acc[...] += ge * out

    @pl.when(e == pl.num_programs(1) - 1)
    def _():
        y_ref[...] = x1_ref[...] + gt2_ref[...] * acc[...]


def _moe(h2, gates, x1, gt2, wg_bf, wu_bf, wd_bf, tm):
    rows = h2.shape[0]
    per_row = gt2.shape[0] != 1
    mod_spec = (pl.BlockSpec((tm, D_MODEL), lambda i, e: (i, 0)) if per_row
                else pl.BlockSpec((1, D_MODEL), lambda i, e: (0, 0)))
    row_spec = lambda width: pl.BlockSpec((tm, width), lambda i, e: (i, 0))
    return pl.pallas_call(
        _moe_kernel,
        grid=(rows // tm, N_EXPERTS),
        in_specs=[row_spec(D_MODEL), row_spec(ROUTER_LANES), row_spec(D_MODEL), mod_spec,
                  pl.BlockSpec((1, D_MODEL, D_EXPERT), lambda i, e: (e, 0, 0)),
                  pl.BlockSpec((1, D_MODEL, D_EXPERT), lambda i, e: (e, 0, 0)),
                  pl.BlockSpec((1, D_EXPERT, D_MODEL), lambda i, e: (e, 0, 0))],
        out_specs=row_spec(D_MODEL),
        out_shape=jax.ShapeDtypeStruct((rows, D_MODEL), F32),
        scratch_shapes=[pltpu.VMEM((tm, D_MODEL), F32)],
        compiler_params=_params("parallel", "arbitrary"),
        name="moe",
    )(h2, gates, x1, gt2, wg_bf, wu_bf, wd_bf)


def _iota_f32(shape, dim):
    return lax.broadcasted_iota(jnp.int32, shape, dim).astype(F32)


def _one_or_zero(a, b):
    return jnp.where(a, 1.0, jnp.where(b, 1.0, 0.0))


def _moe_sorted_kernel(h_ref, route_ref, route_t_ref, wg_ref, wu_ref, wd_ref, o_ref,
                       xs, out_hi, out_lo, ws, d1c_s, d2c_s, meta):
    s = pl.program_id(1)
    tb = h_ref.shape[0]
    pr = xs.shape[0] - MOE_TR

    @pl.when(s == 0)
    def _():
        rt = route_t_ref[...]
        i1r, i2r, g1r, g2r = rt[0:1], rt[1:2], rt[2:3], rt[3:4]
        eio = _iota_f32((ROUTER_LANES, tb), 0)
        s1t = eio == i1r
        s2t = eio == i2r
        mt = _one_or_zero(s1t, s2t).astype(BF16)
        ranks = []
        for c in range(tb // MOE_TR):
            before = _iota_f32((tb, MOE_TR), 0) < _iota_f32((tb, MOE_TR), 1) + float(c * MOE_TR)
            ranks.append(jnp.dot(mt, jnp.where(before, 1.0, 0.0).astype(BF16), preferred_element_type=F32))
        rank_t = jnp.concatenate(ranks, axis=1)
        cnt_col = jnp.sum(mt.astype(F32), axis=1, keepdims=True)
        seg_col = jnp.floor((cnt_col + (MOE_ALIGN - 1.0)) * (1.0 / MOE_ALIGN))
        lower = jnp.where(_iota_f32((ROUTER_LANES, ROUTER_LANES), 1) < _iota_f32((ROUTER_LANES, ROUTER_LANES), 0),
                          1.0, 0.0).astype(BF16)
        off_col = MOE_ALIGN * jnp.dot(lower, jnp.broadcast_to(seg_col, (ROUTER_LANES, ROUTER_LANES)).astype(BF16),
                                      preferred_element_type=F32)[:, 0:1]
        dest_t = off_col + rank_t
        d1r = jnp.sum(jnp.where(s1t, dest_t, 0.0), axis=0, keepdims=True)
        d2r = jnp.sum(jnp.where(s2t, dest_t, 0.0), axis=0, keepdims=True)

        r = route_ref[...]
        lio = _iota_f32((tb, ROUTER_LANES), 1)
        s1 = lio == r[:, 0:1]
        s2 = lio == r[:, 1:2]
        m = _one_or_zero(s1, s2).astype(BF16)
        ranks = []
        for c in range(tb // MOE_TR):
            before = _iota_f32((MOE_TR, tb), 1) < _iota_f32((MOE_TR, tb), 0) + float(c * MOE_TR)
            ranks.append(jnp.dot(jnp.where(before, 1.0, 0.0).astype(BF16), m, preferred_element_type=F32))
        rank = jnp.concatenate(ranks, axis=0)
        cnt_row = jnp.sum(m.astype(F32), axis=0, keepdims=True)
        seg_row = jnp.floor((cnt_row + (MOE_ALIGN - 1.0)) * (1.0 / MOE_ALIGN))
        upper = jnp.where(_iota_f32((ROUTER_LANES, ROUTER_LANES), 0) < _iota_f32((ROUTER_LANES, ROUTER_LANES), 1),
                          1.0, 0.0).astype(BF16)
        off_row = MOE_ALIGN * jnp.dot(jnp.broadcast_to(seg_row, (SUBLANES, ROUTER_LANES)).astype(BF16), upper,
                                      preferred_element_type=F32)[0:1, :]
        dest = off_row + rank
        d1c_s[...] = jnp.sum(jnp.where(s1, dest, 0.0), axis=-1, keepdims=True)
        d2c_s[...] = jnp.sum(jnp.where(s2, dest, 0.0), axis=-1, keepdims=True)
        off_i = off_row.astype(jnp.int32)
        cnt_i = cnt_row.astype(jnp.int32)
        for e in range(N_EXPERTS):
            meta[e] = off_i[0, e]
            meta[N_EXPERTS + e] = cnt_i[0, e]

        x = h_ref[...]
        for c in range(pr // MOE_CH):
            prow = _iota_f32((MOE_CH, tb), 0) + float(c * MOE_CH)
            h1 = prow == d1r
            h2 = prow == d2r
            xs[c * MOE_CH:(c + 1) * MOE_CH, :] = jnp.dot(_one_or_zero(h1, h2).astype(BF16), x,
                                                       preferred_element_type=F32).astype(BF16)
            ws[c * MOE_CH:(c + 1) * MOE_CH, :] = jnp.sum(jnp.where(h1, g1r, 0.0) + jnp.where(h2, g2r, 0.0),
                                                       axis=-1, keepdims=True)
        xs[pr:pr + MOE_TR, :] = jnp.zeros((MOE_TR, D_MODEL), BF16)
        ws[pr:pr + MOE_TR, :] = jnp.zeros((MOE_TR, 1), F32)
        out_hi[...] = jnp.zeros(out_hi.shape, BF16)
        out_lo[...] = jnp.zeros(out_lo.shape, BF16)

    for k in range(MOE_EXPERTS_PER_STEP):
        e = s * MOE_EXPERTS_PER_STEP + k
        off = meta[e]
        n_tiles = (meta[N_EXPERTS + e] + (MOE_TR - 1)) // MOE_TR

        @pl.loop(0, n_tiles)
        def _(i):
            r0 = pl.multiple_of(off + i * MOE_TR, MOE_ALIGN)
            rows = xs[pl.ds(r0, MOE_TR), :]
            g = jnp.dot(rows, wg_ref[k], preferred_element_type=F32)
            u = jnp.dot(rows, wu_ref[k], preferred_element_type=F32)
            he = (g * jax.nn.sigmoid(g)) * u
            out = jnp.dot(he.astype(BF16), wd_ref[k], preferred_element_type=F32) * ws[pl.ds(r0, MOE_TR), :]
            hi = out.astype(BF16)
            out_hi[pl.ds(r0, MOE_TR), :] = hi
            out_lo[pl.ds(r0, MOE_TR), :] = (out - hi.astype(F32)).astype(BF16)

    @pl.when(s == pl.num_programs(1) - 1)
    def _():
        d1c = d1c_s[...]
        d2c = d2c_s[...]
        acc = jnp.zeros((tb, D_MODEL), F32)
        for c in range(pr // MOE_CH):
            pcol = _iota_f32((tb, MOE_CH), 1) + float(c * MOE_CH)
            pt = _one_or_zero(pcol == d1c, pcol == d2c).astype(BF16)
            acc = (acc + jnp.dot(pt, out_hi[c * MOE_CH:(c + 1) * MOE_CH, :], preferred_element_type=F32)
                   + jnp.dot(pt, out_lo[c * MOE_CH:(c + 1) * MOE_CH, :], preferred_element_type=F32))
        o_ref[...] = acc


def _moe_sorted(h2, route, route_t, wg_bf, wu_bf, wd_bf):
    rows = h2.shape[0]
    tb = MOE_TB
    pr = -(-(2 * tb + N_EXPERTS * (MOE_ALIGN - 1)) // MOE_CH) * MOE_CH
    eps = MOE_EXPERTS_PER_STEP
    return pl.pallas_call(
        _moe_sorted_kernel,
        grid=(rows // tb, N_EXPERTS // eps),
        in_specs=[pl.BlockSpec((tb, D_MODEL), lambda b, s: (b, 0)),
                  pl.BlockSpec((tb, ROUTER_LANES), lambda b, s: (b, 0)),
                  pl.BlockSpec((SUBLANES, tb), lambda b, s: (0, b)),
                  pl.BlockSpec((eps, D_MODEL, D_EXPERT), lambda b, s: (s, 0, 0)),
                  pl.BlockSpec((eps, D_MODEL, D_EXPERT), lambda b, s: (s, 0, 0)),
                  pl.BlockSpec((eps, D_EXPERT, D_MODEL), lambda b, s: (s, 0, 0))],
        out_specs=pl.BlockSpec((tb, D_MODEL), lambda b, s: (b, 0)),
        out_shape=jax.ShapeDtypeStruct((rows, D_MODEL), F32),
        scratch_shapes=[pltpu.VMEM((pr + MOE_TR, D_MODEL), BF16), pltpu.VMEM((pr + MOE_TR, D_MODEL), BF16),
                        pltpu.VMEM((pr + MOE_TR, D_MODEL), BF16), pltpu.VMEM((pr + MOE_TR, 1), F32),
                        pltpu.VMEM((tb, 1), F32), pltpu.VMEM((tb, 1), F32),
                        pltpu.SMEM((2 * N_EXPERTS,), jnp.int32)],
        compiler_params=pltpu.CompilerParams(dimension_semantics=("parallel", "arbitrary"),
                                             vmem_limit_bytes=MOE_VMEM_LIMIT),
        name="moe_sorted",
    )(h2, route, route_t, wg_bf, wu_bf, wd_bf)


def _residual_kernel(x_ref, g_ref, d_ref, o_ref):
    o_ref[...] = x_ref[...] + g_ref[...] * d_ref[...]


def _residual(x1, gt2, delta, tm):
    rows = x1.shape[0]
    row_spec = pl.BlockSpec((tm, D_MODEL), lambda i: (i, 0))
    return pl.pallas_call(
        _residual_kernel,
        grid=(rows // tm,),
        in_specs=[row_spec, pl.BlockSpec((1, D_MODEL), lambda i: (0, 0)), row_spec],
        out_specs=row_spec,
        out_shape=jax.ShapeDtypeStruct((rows, D_MODEL), F32),
        compiler_params=_params("parallel"),
        name="residual",
    )(x1, gt2, delta)


def _rope_tables(pos):
    inv = 1.0 / (ROPE_THETA ** (jnp.arange(0, HEAD_DIM, 2, dtype=F32) / HEAD_DIM))
    ang = pos.astype(F32)[:, None] * inv[None, :]
    cos = jnp.cos(ang)
    sin = jnp.sin(ang)
    reps = LANES // (HEAD_DIM // 2)
    cos_t = jnp.tile(cos, (1, reps))
    sin_t = jnp.tile(jnp.concatenate([-sin, sin], axis=-1), (1, reps // 2))
    return cos_t, sin_t


def _lambda_init(layer):
    return 0.8 - 0.6 * math.exp(-0.3 * layer)


def kernel(x_prompt, x_sample, cache_k, cache_v, state_conv, page_table, c_prompt, c_sample, w_ada, b_ada, g_norm1, g_norm2, w_in, g_qnorm, g_knorm, lam_q1, lam_k1, lam_q2, lam_k2, g_subln, w_dw, b_dw, g_ln_conv, b_ln_conv, w_out, w_router_group, b_router_group, w_router_expert, b_router_expert, w_gate_e, w_up_e, w_down_e):
    depth = w_ada.shape[0]
    assert depth == 1, "single-layer trunk"
    layer = 0
    batch_p, seq, _ = x_prompt.shape
    batch_s, n_new, _ = x_sample.shape
    assert batch_p == 1
    past = page_table.shape[1] * PAGE_SIZE
    n_pool = cache_k.shape[1]
    lam_init = _lambda_init(layer)
    row = lambda a: a[layer].reshape(1, -1)

    n_c = batch_p + batch_s
    n_c_pad = -(-n_c // SUBLANES) * SUBLANES
    c_all = jnp.concatenate([c_prompt, c_sample, jnp.zeros((n_c_pad - n_c, D_MODEL), F32)], axis=0)
    mod = _ada(c_all, w_ada[layer], b_ada[layer])
    mod_p = [mod[0:1, i * D_MODEL:(i + 1) * D_MODEL] for i in range(N_MOD)]
    mod_s = [jnp.repeat(mod[1:1 + batch_s, i * D_MODEL:(i + 1) * D_MODEL], n_new, axis=0) for i in range(N_MOD)]

    w_in_bf = w_in[layer].astype(BF16)
    w_out_bf = w_out[layer].astype(BF16)
    wg_bf = w_gate_e[layer].astype(BF16)
    wu_bf = w_up_e[layer].astype(BF16)
    wd_bf = w_down_e[layer].astype(BF16)
    gq = jnp.tile(g_qnorm[layer], QK_WIDTH // HEAD_DIM).reshape(1, -1)
    gk = jnp.tile(g_knorm[layer], QK_WIDTH // HEAD_DIM).reshape(1, -1)
    seg_id = jnp.arange(QK_WIDTH) // HEAD_DIM
    seg = (seg_id[:, None] == seg_id[None, :]).astype(BF16)
    w_r = jnp.concatenate([w_router_expert[layer], w_router_group[layer],
                           jnp.zeros((D_MODEL, ROUTER_LANES - N_EXPERTS - N_GROUPS), F32)], axis=1)
    wr_hi = w_r.astype(BF16)
    wr_lo = (w_r - wr_hi.astype(F32)).astype(BF16)
    b_r = jnp.concatenate([b_router_expert[layer], b_router_group[layer],
                           jnp.zeros((ROUTER_LANES - N_EXPERTS - N_GROUPS,), F32)]).reshape(1, -1)
    lam_p = [row(lam_q1), row(lam_k1), row(lam_q2), row(lam_k2)]
    gs = row(g_subln)
    conv_p = (w_dw[layer], row(b_dw), row(g_ln_conv), row(b_ln_conv))

    xp = x_prompt.reshape(seq, D_MODEL)
    cos_p, sin_p = _rope_tables(jnp.arange(seq))
    k_p, v_p, glu_p, q_bf, k_bf, v_bf = _in_proj(xp, mod_p[0], mod_p[1], row(g_norm1), w_in_bf, gq, gk,
                                                 cos_p, sin_p, seg, tm=512)
    oc_p = _conv_prompt(glu_p, *conv_p, tm=512)
    oa_p = _attn_prompt(q_bf, k_bf, v_bf, lam_p, gs, lam_init, tq=512, tk=1024)
    x1_p, h2_p, _, route_p = _out_router(xp, oa_p, oc_p, w_out_bf, mod_p[2], mod_p[3], mod_p[4], row(g_norm2),
                                         wr_hi, wr_lo, b_r, tm=512)
    delta_p = _moe_sorted(h2_p, route_p, jnp.transpose(route_p[:, :SUBLANES]), wg_bf, wu_bf, wd_bf)
    y_p = _residual(x1_p, mod_p[5], delta_p, tm=512)

    xs = x_sample.reshape(batch_s * n_new, D_MODEL)
    pos_s = past + (jnp.arange(batch_s * n_new) % n_new)
    cos_s, sin_s = _rope_tables(pos_s)
    rows_s = batch_s * n_new
    k_s, v_s, glu_s, qs_bf, ks_bf, vs_bf = _in_proj(xs, mod_s[0], mod_s[1], row(g_norm1), w_in_bf, gq, gk,
                                                    cos_s, sin_s, seg, tm=rows_s)
    xp_s = jnp.concatenate([state_conv[layer], glu_s.reshape(batch_s, n_new, CONV_CH)], axis=1)
    oc_s = _conv_sample(jnp.transpose(xp_s, (1, 0, 2)), *conv_p, n_new=n_new)
    oc_s = jnp.transpose(oc_s, (1, 0, 2)).reshape(rows_s, CONV_CH)
    tok3 = lambda a: a.reshape(batch_s, n_new, D_ATTN)
    pad_q = lambda a: jnp.pad(tok3(a), ((0, 0), (0, SUBLANES - n_new), (0, 0)))
    oa_s = _attn_sample(page_table, _block_diag_queries(tok3(qs_bf), SUBLANES), pad_q(k_s), pad_q(v_s),
                        jnp.transpose(cache_k[layer], (0, 2, 3, 4, 1)).reshape(n_pool, D_ATTN, PAGE_SIZE),
                        cache_v[layer].reshape(n_pool, PAGE_SIZE * N_HEADS, V_DIM), lam_p, gs, lam_init, n_new)
    oa_s = oa_s[:, :n_new].reshape(rows_s, D_ATTN)
    x1_s, h2_s, gates_s, _ = _out_router(xs, oa_s, oc_s, w_out_bf, mod_s[2], mod_s[3],
                                      mod_s[4], row(g_norm2), wr_hi, wr_lo, b_r, tm=rows_s)
    y_s = _moe(h2_s, gates_s, x1_s, mod_s[5], wg_bf, wu_bf, wd_bf, tm=rows_s)

    hshape = (N_HEADS, 2, HEAD_DIM)
    return (y_p.reshape(batch_p, seq, D_MODEL),
            y_s.reshape(batch_s, n_new, D_MODEL),
            k_p.reshape(depth, batch_p, seq, *hshape),
            v_p.reshape(depth, batch_p, seq, N_HEADS, V_DIM),
            glu_p[seq - (CONV_WIDTH - 1):].reshape(depth, batch_p, CONV_WIDTH - 1, CONV_CH),
            k_s.reshape(depth, batch_s, n_new, *hshape),
            v_s.reshape(depth, batch_s, n_new, N_HEADS, V_DIM),
            xp_s[:, n_new:].reshape(depth, batch_s, CONV_WIDTH - 1, CONV_CH))
```

```python
import functools
import math

import jax
import jax.numpy as jnp
from jax import lax
from jax.experimental import pallas as pl
from jax.experimental.pallas import tpu as pltpu

F32 = jnp.float32
BF16 = jnp.bfloat16

D_MODEL = 1024
HEAD_DIM = 64
V_DIM = 2 * HEAD_DIM
N_HEADS = (D_MODEL // 2) // V_DIM
D_ATTN = N_HEADS * V_DIM
QK_WIDTH = N_HEADS * 2 * HEAD_DIM
CONV_CH = D_MODEL - D_ATTN
CONV_WIDTH = 31
IN_COLS = 2 * QK_WIDTH + D_ATTN + 2 * CONV_CH
N_GROUPS = 4
EXPERTS_PER_GROUP = 8
N_EXPERTS = N_GROUPS * EXPERTS_PER_GROUP
D_EXPERT = D_MODEL // 4
ROPE_THETA = 10000.0
PAGE_SIZE = 128
EPS = 1e-6
N_MOD = 6
NEG_INF = -1e30
QK_SCALE_LOG2 = HEAD_DIM ** -0.5 * math.log2(math.e)

LANES = 128
SUBLANES = 8
VMEM_LIMIT = 48 * 1024 * 1024
HALO = 32
CONV_CHUNK = 64
PAGES_PER_STEP = 32
ROUTER_LANES = 128
MOE_TB = 1024
MOE_TR = 256
MOE_CH = 512
MOE_ALIGN = 16
MOE_EXPERTS_PER_STEP = 2
MOE_VMEM_LIMIT = 56 * 1024 * 1024


def _params(*sem):
    return pltpu.CompilerParams(dimension_semantics=sem, vmem_limit_bytes=VMEM_LIMIT)


def _nt_dot(a, b):
    return lax.dot_general(a, b, (((1,), (1,)), ((), ())), preferred_element_type=F32)


def _ada_kernel(c_ref, w_ref, b_ref, o_ref):
    c = c_ref[...]
    a = (c * jax.nn.sigmoid(c)).astype(BF16)
    o_ref[...] = jnp.dot(a, w_ref[...].astype(BF16), preferred_element_type=F32) + b_ref[...]


def _ada(c_all, w_ada, b_ada):
    rows = c_all.shape[0]
    return pl.pallas_call(
        _ada_kernel,
        grid=(N_MOD,),
        in_specs=[
            pl.BlockSpec((rows, D_MODEL), lambda j: (0, 0)),
            pl.BlockSpec((D_MODEL, D_MODEL), lambda j: (0, j)),
            pl.BlockSpec((1, D_MODEL), lambda j: (0, j)),
        ],
        out_specs=pl.BlockSpec((rows, D_MODEL), lambda j: (0, j)),
        out_shape=jax.ShapeDtypeStruct((rows, N_MOD * D_MODEL), F32),
        compiler_params=_params("arbitrary"),
        name="ada",
    )(c_all, w_ada, b_ada.reshape(1, -1))


def _rms_rows(x, g):
    return x * lax.rsqrt(jnp.mean(x * x, axis=-1, keepdims=True) + EPS) * g


def _qk_norm_rope(t, g, seg, cos, sin_signed, first_half):
    ms = jnp.dot((t * t).astype(BF16), seg, preferred_element_type=F32) * (1.0 / HEAD_DIM)
    tn = t * lax.rsqrt(ms + EPS) * g
    out = []
    for i in range(t.shape[1] // LANES):
        xs = tn[:, i * LANES:(i + 1) * LANES]
        rot = jnp.where(first_half, pltpu.roll(xs, LANES - HEAD_DIM // 2, 1), pltpu.roll(xs, HEAD_DIM // 2, 1))
        out.append(xs * cos + rot * sin_signed)
    return jnp.concatenate(out, axis=1)


def _in_proj_kernel(x_ref, sh_ref, sc_ref, g1_ref, w_ref, gq_ref, gk_ref, cos_ref, sin_ref, seg_ref,
                    k_out, v_out, glu_out, q_bf, k_bf, v_bf):
    x = x_ref[...]
    h = _rms_rows(x, g1_ref[...]) * (1.0 + sc_ref[...]) + sh_ref[...]
    proj = jnp.dot(h.astype(BF16), w_ref[...], preferred_element_type=F32)
    cos = cos_ref[...]
    sin = sin_ref[...]
    seg = seg_ref[...]
    lane = lax.broadcasted_iota(jnp.int32, cos.shape, 1)
    first_half = (lane % HEAD_DIM) < (HEAD_DIM // 2)
    q = _qk_norm_rope(proj[:, :QK_WIDTH], gq_ref[...], seg, cos, sin, first_half)
    k = _qk_norm_rope(proj[:, QK_WIDTH:2 * QK_WIDTH], gk_ref[...], seg, cos, sin, first_half)
    v = proj[:, 2 * QK_WIDTH:2 * QK_WIDTH + D_ATTN]
    a = proj[:, 2 * QK_WIDTH + D_ATTN:2 * QK_WIDTH + D_ATTN + CONV_CH]
    gate = proj[:, 2 * QK_WIDTH + D_ATTN + CONV_CH:]
    k_out[...] = k
    v_out[...] = v
    glu_out[...] = a * jax.nn.sigmoid(gate)
    q_bf[...] = (q * QK_SCALE_LOG2).astype(BF16)
    k_bf[...] = k.astype(BF16)
    v_bf[...] = v.astype(BF16)


def _in_proj(x, sh, sc, g1, w_in_bf, gq, gk, cos, sin, seg, tm):
    rows = x.shape[0]
    per_row = sh.shape[0] != 1
    mod_spec = pl.BlockSpec((tm, D_MODEL), lambda i: (i, 0)) if per_row else pl.BlockSpec((1, D_MODEL), lambda i: (0, 0))
    const = lambda shape: pl.BlockSpec(shape, lambda i: (0, 0))
    row_spec = lambda width: pl.BlockSpec((tm, width), lambda i: (i, 0))
    f32_out = jax.ShapeDtypeStruct((rows, QK_WIDTH), F32)
    bf_out = jax.ShapeDtypeStruct((rows, QK_WIDTH), BF16)
    return pl.pallas_call(
        _in_proj_kernel,
        grid=(rows // tm,),
        in_specs=[row_spec(D_MODEL), mod_spec, mod_spec, const((1, D_MODEL)), const((D_MODEL, IN_COLS)),
                  const((1, QK_WIDTH)), const((1, QK_WIDTH)), row_spec(LANES), row_spec(LANES),
                  const((QK_WIDTH, QK_WIDTH))],
        out_specs=[row_spec(QK_WIDTH)] * 6,
        out_shape=[f32_out, f32_out, f32_out, bf_out, bf_out, bf_out],
        compiler_params=_params("parallel"),
        name="in_proj",
    )(x, sh, sc, g1, w_in_bf, gq, gk, cos, sin, seg)


def _ln_swish(y, g, b):
    mu = jnp.mean(y, axis=-1, keepdims=True)
    d = y - mu
    var = jnp.mean(d * d, axis=-1, keepdims=True)
    z = d * lax.rsqrt(var + EPS) * g + b
    return z * jax.nn.sigmoid(z)


def _conv_prompt_kernel(glu_ref, w_ref, bdw_ref, g_ref, b_ref, o_ref, xpad, shifted):
    tm = glu_ref.shape[0]

    @pl.when(pl.program_id(0) == 0)
    def _():
        xpad[0:HALO, :] = jnp.zeros((HALO, CONV_CH), F32)

    xpad[HALO:HALO + tm, :] = glu_ref[...]
    base = HALO - (CONV_WIDTH - 1)
    for c in range(tm // CONV_CHUNK):
        r0 = c * CONV_CHUNK
        acc = jnp.zeros((CONV_CHUNK, CONV_CH), F32)
        for phase in range(SUBLANES):
            taps = [j for j in range(CONV_WIDTH) if (base + j) % SUBLANES == phase]
            reach = max(base + j - phase for j in taps) + CONV_CHUNK
            shifted[phase, 0:reach, :] = xpad[r0 + phase:r0 + phase + reach, :]
            for j in taps:
                a = base + j - phase
                acc = acc + shifted[phase, a:a + CONV_CHUNK, :] * w_ref[j:j + 1, :]
        y = acc + bdw_ref[...]
        o_ref[r0:r0 + CONV_CHUNK, :] = _ln_swish(y, g_ref[...], b_ref[...]).astype(BF16)
    xpad[0:HALO, :] = xpad[tm:tm + HALO, :]


def _conv_prompt(glu, w_dw, b_dw, g_ln, b_ln, tm):
    rows = glu.shape[0]
    const = lambda shape: pl.BlockSpec(shape, lambda i: (0, 0))
    return pl.pallas_call(
        _conv_prompt_kernel,
        grid=(rows // tm,),
        in_specs=[pl.BlockSpec((tm, CONV_CH), lambda i: (i, 0)), const((CONV_WIDTH, CONV_CH)),
                  const((1, CONV_CH)), const((1, CONV_CH)), const((1, CONV_CH))],
        out_specs=pl.BlockSpec((tm, CONV_CH), lambda i: (i, 0)),
        out_shape=jax.ShapeDtypeStruct((rows, CONV_CH), BF16),
        scratch_shapes=[pltpu.VMEM((tm + HALO, CONV_CH), F32),
                        pltpu.VMEM((SUBLANES, CONV_CHUNK + HALO, CONV_CH), F32)],
        compiler_params=_params("arbitrary"),
        name="conv_prompt",
    )(glu, w_dw, b_dw, g_ln, b_ln)


def _conv_sample_kernel(xp_ref, w_ref, bdw_ref, g_ref, b_ref, o_ref):
    n_new = o_ref.shape[0]
    for i in range(n_new):
        acc = jnp.zeros(o_ref.shape[1:], F32)
        for j in range(CONV_WIDTH):
            acc = acc + xp_ref[i + j] * w_ref[j:j + 1, :]
        o_ref[i] = _ln_swish(acc + bdw_ref[...], g_ref[...], b_ref[...]).astype(BF16)


def _conv_sample(xp_t, w_dw, b_dw, g_ln, b_ln, n_new):
    t, batch, ch = xp_t.shape
    const2 = lambda shape: pl.BlockSpec(shape, lambda i: (0, 0))
    return pl.pallas_call(
        _conv_sample_kernel,
        grid=(1,),
        in_specs=[pl.BlockSpec((t, batch, ch), lambda i: (0, 0, 0)), const2((CONV_WIDTH, ch)),
                  const2((1, ch)), const2((1, ch)), const2((1, ch))],
        out_specs=pl.BlockSpec((n_new, batch, ch), lambda i: (0, 0, 0)),
        out_shape=jax.ShapeDtypeStruct((n_new, batch, ch), BF16),
        compiler_params=_params("arbitrary"),
        name="conv_sample",
    )(xp_t, w_dw, b_dw, g_ln, b_ln)


def _lambda(lq1, lk1, lq2, lk2, lam_init):
    s1 = jnp.sum(lq1[...] * lk1[...], axis=-1, keepdims=True)
    s2 = jnp.sum(lq2[...] * lk2[...], axis=-1, keepdims=True)
    return jnp.exp(s1) - jnp.exp(s2) + lam_init


def _online_step(s, vb, m, l, acc):
    m_new = jnp.maximum(m, jnp.max(s, axis=-1, keepdims=True))
    alpha = jnp.exp2(m - m_new)
    e = jnp.exp2(s - m_new)
    l = alpha * l + jnp.sum(e, axis=-1, keepdims=True)
    acc = alpha * acc + jnp.dot(e.astype(BF16), vb, preferred_element_type=F32)
    return m_new, l, acc


def _attn_prompt_kernel(q_ref, k_ref, v_ref, lq1, lk1, lq2, lk2, gs_ref, o_ref, m_scr, l_scr, acc_scr, lp_scr,
                        kmax_scr, *, lam_init, tk):
    tq = q_ref.shape[0]
    qi = pl.program_id(1)
    q = q_ref[...]
    lane = lax.broadcasted_iota(jnp.int32, q.shape, 1)
    zero = jnp.zeros_like(q)
    q2 = jnp.concatenate([jnp.where(lane < HEAD_DIM, q, zero), jnp.where(lane >= HEAD_DIM, q, zero)], axis=0)

    @pl.when(qi == 0)
    def _():
        def chunk_max(c, carry):
            kb = k_ref[pl.ds(pl.multiple_of(c * tk, tk), tk), :].astype(F32)
            kk = kb * kb
            lane_k = lax.broadcasted_iota(jnp.int32, kk.shape, 1)
            n0 = jnp.sum(jnp.where(lane_k < HEAD_DIM, kk, 0.0), axis=-1, keepdims=True)
            n1 = jnp.sum(jnp.where(lane_k >= HEAD_DIM, kk, 0.0), axis=-1, keepdims=True)
            return (jnp.maximum(carry[0], jnp.max(n0, axis=0, keepdims=True)),
                    jnp.maximum(carry[1], jnp.max(n1, axis=0, keepdims=True)))

        k0, k1 = lax.fori_loop(0, k_ref.shape[0] // tk, chunk_max, (jnp.zeros((1, 1), F32), jnp.zeros((1, 1), F32)))
        kmax_scr[0:1, :] = jnp.broadcast_to(k0, (1, LANES))
        kmax_scr[1:2, :] = jnp.broadcast_to(k1, (1, LANES))

    qf = q2.astype(F32)
    kmax = jnp.concatenate([jnp.broadcast_to(kmax_scr[0:1, 0:1], (tq, 1)),
                            jnp.broadcast_to(kmax_scr[1:2, 0:1], (tq, 1))], axis=0)
    bound = jnp.sqrt(jnp.sum(qf * qf, axis=-1, keepdims=True) * kmax)

    def scores(j, masked):
        off = pl.multiple_of(j * tk, tk)
        s = _nt_dot(q2, k_ref[pl.ds(off, tk), :])
        if masked:
            row = lax.broadcasted_iota(jnp.int32, s.shape, 0) % tq + qi * tq
            col = lax.broadcasted_iota(jnp.int32, s.shape, 1) + off
            s = jnp.where(col <= row, s, NEG_INF)
        return s, v_ref[pl.ds(off, tk), :]

    def fast(j, masked):
        s, vb = scores(j, masked)
        e = jnp.exp2(s - bound)
        part = e[:, 0:LANES]
        for t in range(1, tk // LANES):
            part = part + e[:, t * LANES:(t + 1) * LANES]
        lp_scr[...] += part
        acc_scr[...] += jnp.dot(e.astype(BF16), vb, preferred_element_type=F32)

    def exact(j, masked):
        s, vb = scores(j, masked)
        m, l, acc = _online_step(s, vb, m_scr[...], l_scr[...], acc_scr[...])
        m_scr[...] = m
        l_scr[...] = l
        acc_scr[...] = acc

    def finish(l):
        lam = _lambda(lq1, lk1, lq2, lk2, lam_init)
        acc = acc_scr[...]
        o = acc[:tq] / l[:tq] - lam * (acc[tq:] / l[tq:])
        o_ref[...] = (_rms_rows(o, gs_ref[...]) * (1.0 - lam_init)).astype(BF16)

    n_full = (qi * tq) // tk
    n_pairs = n_full // 2
    last = 2 * n_pairs
    lp_scr[...] = jnp.zeros(lp_scr.shape, F32)
    acc_scr[...] = jnp.zeros(acc_scr.shape, F32)

    @pl.loop(0, n_pairs)
    def _(p):
        fast(2 * p, False)
        fast(2 * p + 1, False)

    @pl.when(n_full == last)
    def _():
        fast(last, True)

    @pl.when(n_full != last)
    def _():
        fast(last, False)
        fast(last + 1, True)

    l_fast = jnp.sum(lp_scr[...], axis=-1, keepdims=True)
    trusted = jnp.logical_and(jnp.min(l_fast) > 2.0 ** -60, jnp.max(l_fast) < 2.0 ** 100)

    @pl.when(trusted)
    def _():
        finish(l_fast)

    @pl.when(jnp.logical_not(trusted))
    def _():
        m_scr[...] = jnp.full(m_scr.shape, NEG_INF, F32)
        l_scr[...] = jnp.zeros(l_scr.shape, F32)
        acc_scr[...] = jnp.zeros(acc_scr.shape, F32)

        @pl.loop(0, n_full)
        def _(j):
            exact(j, False)

        exact(n_full, True)
        finish(l_scr[...])


def _attn_prompt(q_bf, k_bf, v_bf, lam_p, g_subln, lam_init, tq, tk):
    s = q_bf.shape[0]
    assert tk % tq == 0 and s % tk == 0
    vec = lambda n: pl.BlockSpec((1, n), lambda h, i: (0, 0))
    return pl.pallas_call(
        functools.partial(_attn_prompt_kernel, lam_init=lam_init, tk=tk),
        grid=(N_HEADS, s // tq),
        in_specs=[pl.BlockSpec((tq, V_DIM), lambda h, i: (i, h)),
                  pl.BlockSpec((s, V_DIM), lambda h, i: (0, h)),
                  pl.BlockSpec((s, V_DIM), lambda h, i: (0, h)),
                  vec(HEAD_DIM), vec(HEAD_DIM), vec(HEAD_DIM), vec(HEAD_DIM), vec(V_DIM)],
        out_specs=pl.BlockSpec((tq, V_DIM), lambda h, i: (i, h)),
        out_shape=jax.ShapeDtypeStruct((s, D_ATTN), BF16),
        scratch_shapes=[pltpu.VMEM((2 * tq, 1), F32), pltpu.VMEM((2 * tq, 1), F32),
                        pltpu.VMEM((2 * tq, V_DIM), F32), pltpu.VMEM((2 * tq, LANES), F32),
                        pltpu.VMEM((SUBLANES, LANES), F32)],
        compiler_params=_params("arbitrary", "arbitrary"),
        name="attn_prompt",
    )(q_bf, k_bf, v_bf, *lam_p, g_subln)


def _attn_sample_kernel(pt_ref, qbd_ref, kn_ref, vn_ref, *rest, lam_init, n_new):
    del pt_ref
    npg = PAGES_PER_STEP
    kp = rest[:npg]
    vp = rest[npg:2 * npg]
    lq1, lk1, lq2, lk2, gs_ref, o_ref, m_s, l_s, acc_s = rest[2 * npg:]
    j = pl.program_id(1)
    n_rows = qbd_ref.shape[1]
    hrows = n_rows // N_HEADS
    qpad = hrows // 2

    @pl.when(j == 0)
    def _():
        m_s[...] = jnp.full(m_s.shape, NEG_INF, F32)
        l_s[...] = jnp.zeros(l_s.shape, F32)
        acc_s[...] = jnp.zeros(acc_s.shape, F32)

    qbd = qbd_ref[0]
    kc = jnp.concatenate([r[0] for r in kp], axis=1).astype(BF16)
    s = jnp.dot(qbd, kc, preferred_element_type=F32)
    m = m_s[...]
    m_new = jnp.maximum(m, jnp.max(s, axis=-1, keepdims=True))
    alpha = jnp.exp2(m - m_new)
    e = jnp.exp2(s - m_new)
    l = alpha * l_s[...] + jnp.sum(e, axis=-1, keepdims=True)
    e = e.astype(BF16)
    pv = []
    for h in range(N_HEADS):
        vh = jnp.concatenate([r[0, pl.ds(h, PAGE_SIZE, stride=N_HEADS), :] for r in vp], axis=0).astype(BF16)
        pv.append(jnp.dot(e[h * hrows:(h + 1) * hrows], vh, preferred_element_type=F32))
    acc = alpha * acc_s[...] + jnp.concatenate(pv, axis=0)
    m_s[...] = m_new
    l_s[...] = l
    acc_s[...] = acc

    @pl.when(j == pl.num_programs(1) - 1)
    def _():
        qf = qbd.astype(F32)
        kn = kn_ref[0].astype(BF16).astype(F32)
        vn = vn_ref[0].astype(BF16).astype(F32)
        qpos = lax.broadcasted_iota(jnp.int32, (n_rows, 1), 0) % qpad
        s_new = [jnp.where(t <= qpos, jnp.sum(qf * kn[t:t + 1, :], axis=-1, keepdims=True), NEG_INF)
                 for t in range(n_new)]
        m_f = m_new
        for s_t in s_new:
            m_f = jnp.maximum(m_f, s_t)
        a_f = jnp.exp2(m_new - m_f)
        l_f = a_f * l
        acc_f = a_f * acc
        e_new = [jnp.exp2(s_t - m_f) for s_t in s_new]
        for e_t in e_new:
            l_f = l_f + e_t
        lam = _lambda(lq1, lk1, lq2, lk2, lam_init)
        gs = gs_ref[...]
        outs = []
        for h in range(N_HEADS):
            a_h = acc_f[h * hrows:(h + 1) * hrows]
            for t in range(n_new):
                e_t = e_new[t][h * hrows:(h + 1) * hrows].astype(BF16).astype(F32)
                a_h = a_h + e_t * vn[t:t + 1, h * V_DIM:(h + 1) * V_DIM]
            o_h = a_h / l_f[h * hrows:(h + 1) * hrows]
            outs.append(_rms_rows(o_h[:qpad] - lam * o_h[qpad:], gs))
        o_ref[0] = (jnp.concatenate(outs, axis=1) * (1.0 - lam_init)).astype(BF16)


def _attn_sample(page_table, qbd, kn_pad, vn_pad, cache_kt, cache_v, lam_p, g_subln, lam_init, n_new):
    batch, n_rows, _ = qbd.shape
    qpad = kn_pad.shape[1]
    n_pages = page_table.shape[1]
    steps = n_pages // PAGES_PER_STEP
    pt_flat = page_table.reshape(-1)
    tok = lambda r: pl.BlockSpec((1, r, D_ATTN), lambda b, j, pt: (b, 0, 0))

    def page_spec(p, shape):
        return pl.BlockSpec((1,) + shape, lambda b, j, pt: (pt[b * n_pages + j * PAGES_PER_STEP + p], 0, 0))

    vec = lambda n: pl.BlockSpec((1, n), lambda b, j, pt: (0, 0))
    k_pages = [page_spec(p, (D_ATTN, PAGE_SIZE)) for p in range(PAGES_PER_STEP)]
    v_pages = [page_spec(p, (PAGE_SIZE * N_HEADS, V_DIM)) for p in range(PAGES_PER_STEP)]
    grid_spec = pltpu.PrefetchScalarGridSpec(
        num_scalar_prefetch=1,
        grid=(batch, steps),
        in_specs=[tok(n_rows), tok(qpad), tok(qpad)] + k_pages + v_pages
                 + [vec(HEAD_DIM), vec(HEAD_DIM), vec(HEAD_DIM), vec(HEAD_DIM), vec(V_DIM)],
        out_specs=tok(qpad),
        scratch_shapes=[pltpu.VMEM((n_rows, 1), F32), pltpu.VMEM((n_rows, 1), F32),
                        pltpu.VMEM((n_rows, V_DIM), F32)],
    )
    return pl.pallas_call(
        functools.partial(_attn_sample_kernel, lam_init=lam_init, n_new=n_new),
        grid_spec=grid_spec,
        out_shape=jax.ShapeDtypeStruct((batch, qpad, D_ATTN), BF16),
        compiler_params=_params("parallel", "arbitrary"),
        name="attn_sample",
    )(pt_flat, qbd, kn_pad, vn_pad, *([cache_kt] * PAGES_PER_STEP), *([cache_v] * PAGES_PER_STEP),
      *lam_p, g_subln)


def _block_diag_queries(q_bf, qpad):
    batch, n_new, _ = q_bf.shape
    qp = jnp.pad(q_bf, ((0, 0), (0, qpad - n_new), (0, 0)))
    qt = jnp.tile(qp, (1, 2 * N_HEADS, 1))
    chunk = jnp.arange(2 * N_HEADS * qpad) // qpad
    keep = (jnp.arange(D_ATTN)[None, :] // HEAD_DIM) == chunk[:, None]
    return jnp.where(keep[None], qt, jnp.zeros_like(qt))


def _lane_min_where(cond, lane, width):
    return jnp.min(jnp.where(cond, lane, width), axis=-1, keepdims=True)


def _out_router_kernel(x_ref, oa_ref, oc_ref, w_ref, gt1_ref, sh2_ref, sc2_ref, g2_ref,
                       wr_hi_ref, wr_lo_ref, br_ref, x1_out, h2_out, gates_out, route_out):
    mix = (jnp.dot(oa_ref[...], w_ref[0:D_ATTN, :], preferred_element_type=F32)
           + jnp.dot(oc_ref[...], w_ref[D_ATTN:, :], preferred_element_type=F32))
    x1 = x_ref[...] + gt1_ref[...] * mix
    x1_out[...] = x1
    h2 = _rms_rows(x1, g2_ref[...]) * (1.0 + sc2_ref[...]) + sh2_ref[...]
    h2_out[...] = h2.astype(BF16)

    h_hi = h2.astype(BF16)
    h_lo = (h2 - h_hi.astype(F32)).astype(BF16)
    logits = (jnp.dot(h_hi, wr_hi_ref[...], preferred_element_type=F32)
              + jnp.dot(h_hi, wr_lo_ref[...], preferred_element_type=F32)
              + jnp.dot(h_lo, wr_hi_ref[...], preferred_element_type=F32)) + br_ref[...]
    lane_i = lax.broadcasted_iota(jnp.int32, logits.shape, 1)
    lane = lane_i.astype(F32)
    lg = jnp.where(lane_i >= N_EXPERTS, jnp.where(lane_i < N_EXPERTS + N_GROUPS, logits, NEG_INF), NEG_INF)
    mg = jnp.max(lg, axis=-1, keepdims=True)
    pg_top = 1.0 / jnp.sum(jnp.exp(lg - mg), axis=-1, keepdims=True)
    g_idx = _lane_min_where(lg == mg, lane, float(ROUTER_LANES)) - N_EXPERTS
    lane_group = (lane_i // EXPERTS_PER_GROUP).astype(F32)
    le = jnp.where(lane_i < N_EXPERTS, jnp.where(lane_group == g_idx, logits, NEG_INF), NEG_INF)
    m1 = jnp.max(le, axis=-1, keepdims=True)
    i1 = _lane_min_where(le == m1, lane, float(ROUTER_LANES))
    le2 = jnp.where(lane == i1, NEG_INF, le)
    m2 = jnp.max(le2, axis=-1, keepdims=True)
    i2 = _lane_min_where(le2 == m2, lane, float(ROUTER_LANES))
    r = jnp.exp(m2 - m1)
    w1 = 1.0 / (1.0 + r)
    w2 = r / (1.0 + r)
    g1 = pg_top * w1
    g2 = pg_top * w2
    gates_out[...] = jnp.where(lane == i1, g1, 0.0) + jnp.where(lane == i2, g2, 0.0)
    route_out[...] = jnp.where(lane_i == 0, i1, jnp.where(lane_i == 1, i2, jnp.where(lane_i == 2, g1,
                               jnp.where(lane_i == 3, g2, 0.0))))


def _out_router(x, oa, oc, w_out_bf, gt1, sh2, sc2, g2, wr_hi, wr_lo, br, tm):
    rows = x.shape[0]
    per_row = gt1.shape[0] != 1
    mod_spec = pl.BlockSpec((tm, D_MODEL), lambda i: (i, 0)) if per_row else pl.BlockSpec((1, D_MODEL), lambda i: (0, 0))
    const = lambda shape: pl.BlockSpec(shape, lambda i: (0, 0))
    row_spec = lambda width: pl.BlockSpec((tm, width), lambda i: (i, 0))
    return pl.pallas_call(
        _out_router_kernel,
        grid=(rows // tm,),
        in_specs=[row_spec(D_MODEL), row_spec(D_ATTN), row_spec(CONV_CH), const((D_MODEL, D_MODEL)),
                  mod_spec, mod_spec, mod_spec, const((1, D_MODEL)),
                  const((D_MODEL, ROUTER_LANES)), const((D_MODEL, ROUTER_LANES)), const((1, ROUTER_LANES))],
        out_specs=[row_spec(D_MODEL), row_spec(D_MODEL), row_spec(ROUTER_LANES), row_spec(ROUTER_LANES)],
        out_shape=[jax.ShapeDtypeStruct((rows, D_MODEL), F32), jax.ShapeDtypeStruct((rows, D_MODEL), BF16),
                   jax.ShapeDtypeStruct((rows, ROUTER_LANES), F32), jax.ShapeDtypeStruct((rows, ROUTER_LANES), F32)],
        compiler_params=_params("parallel"),
        name="out_router",
    )(x, oa, oc, w_out_bf, gt1, sh2, sc2, g2, wr_hi, wr_lo, br)


def _moe_kernel(h_ref, gates_ref, x1_ref, gt2_ref, wg_ref, wu_ref, wd_ref, y_ref, acc):
    e = pl.program_id(1)

    @pl.when(e == 0)
    def _():
        acc[...] = jnp.zeros(acc.shape, F32)

    t = h_ref[...]
    g = jnp.dot(t, wg_ref[0], preferred_element_type=F32)
    u = jnp.dot(t, wu_ref[0], preferred_element_type=F32)
    he = (g * jax.nn.sigmoid(g)) * u
    out = jnp.dot(he.astype(BF16), wd_ref[0], preferred_element_type=F32)
    gates = gates_ref[...]
    lane = lax.broadcasted_iota(jnp.int32, gates.shape, 1)
    ge = jnp.sum(jnp.where(lane == e, gates, 0.0), axis=-1, keepdims=True)
    acc[...] += ge * out

    @pl.when(e == pl.num_programs(1) - 1)
    def _():
        y_ref[...] = x1_ref[...] + gt2_ref[...] * acc[...]


def _moe(h2, gates, x1, gt2, wg_bf, wu_bf, wd_bf, tm):
    rows = h2.shape[0]
    per_row = gt2.shape[0] != 1
    mod_spec = (pl.BlockSpec((tm, D_MODEL), lambda i, e: (i, 0)) if per_row
                else pl.BlockSpec((1, D_MODEL), lambda i, e: (0, 0)))
    row_spec = lambda width: pl.BlockSpec((tm, width), lambda i, e: (i, 0))
    return pl.pallas_call(
        _moe_kernel,
        grid=(rows // tm, N_EXPERTS),
        in_specs=[row_spec(D_MODEL), row_spec(ROUTER_LANES), row_spec(D_MODEL), mod_spec,
                  pl.BlockSpec((1, D_MODEL, D_EXPERT), lambda i, e: (e, 0, 0)),
                  pl.BlockSpec((1, D_MODEL, D_EXPERT), lambda i, e: (e, 0, 0)),
                  pl.BlockSpec((1, D_EXPERT, D_MODEL), lambda i, e: (e, 0, 0))],
        out_specs=row_spec(D_MODEL),
        out_shape=jax.ShapeDtypeStruct((rows, D_MODEL), F32),
        scratch_shapes=[pltpu.VMEM((tm, D_MODEL), F32)],
        compiler_params=_params("parallel", "arbitrary"),
        name="moe",
    )(h2, gates, x1, gt2, wg_bf, wu_bf, wd_bf)


def _iota_f32(shape, dim):
    return lax.broadcasted_iota(jnp.int32, shape, dim).astype(F32)


def _one_or_zero(a, b):
    return jnp.where(a, 1.0, jnp.where(b, 1.0, 0.0))


def _moe_sorted_kernel(h_ref, route_ref, route_t_ref, x1_ref, gt2_ref, wg_ref, wu_ref, wd_ref, o_ref,
                       xs, out_hi, out_lo, ws, d1c_s, d2c_s, meta):
    s = pl.program_id(1)
    tb = h_ref.shape[0]
    pr = xs.shape[0] - MOE_TR

    @pl.when(s == 0)
    def _():
        rt = route_t_ref[...]
        i1r, i2r, g1r, g2r = rt[0:1], rt[1:2], rt[2:3], rt[3:4]
        eio = _iota_f32((ROUTER_LANES, tb), 0)
        s1t = eio == i1r
        s2t = eio == i2r
        mt = _one_or_zero(s1t, s2t).astype(BF16)
        ranks = []
        for c in range(tb // MOE_TR):
            before = _iota_f32((tb, MOE_TR), 0) < _iota_f32((tb, MOE_TR), 1) + float(c * MOE_TR)
            ranks.append(jnp.dot(mt, jnp.where(before, 1.0, 0.0).astype(BF16), preferred_element_type=F32))
        rank_t = jnp.concatenate(ranks, axis=1)
        cnt_col = jnp.sum(mt.astype(F32), axis=1, keepdims=True)
        seg_col = jnp.floor((cnt_col + (MOE_ALIGN - 1.0)) * (1.0 / MOE_ALIGN))
        lower = jnp.where(_iota_f32((ROUTER_LANES, ROUTER_LANES), 1) < _iota_f32((ROUTER_LANES, ROUTER_LANES), 0),
                          1.0, 0.0).astype(BF16)
        off_col = MOE_ALIGN * jnp.dot(lower, jnp.broadcast_to(seg_col, (ROUTER_LANES, ROUTER_LANES)).astype(BF16),
                                      preferred_element_type=F32)[:, 0:1]
        dest_t = off_col + rank_t
        d1r = jnp.sum(jnp.where(s1t, dest_t, 0.0), axis=0, keepdims=True)
        d2r = jnp.sum(jnp.where(s2t, dest_t, 0.0), axis=0, keepdims=True)

        r = route_ref[...]
        lio = _iota_f32((tb, ROUTER_LANES), 1)
        s1 = lio == r[:, 0:1]
        s2 = lio == r[:, 1:2]
        m = _one_or_zero(s1, s2).astype(BF16)
        ranks = []
        for c in range(tb // MOE_TR):
            before = _iota_f32((MOE_TR, tb), 1) < _iota_f32((MOE_TR, tb), 0) + float(c * MOE_TR)
            ranks.append(jnp.dot(jnp.where(before, 1.0, 0.0).astype(BF16), m, preferred_element_type=F32))
        rank = jnp.concatenate(ranks, axis=0)
        cnt_row = jnp.sum(m.astype(F32), axis=0, keepdims=True)
        seg_row = jnp.floor((cnt_row + (MOE_ALIGN - 1.0)) * (1.0 / MOE_ALIGN))
        upper = jnp.where(_iota_f32((ROUTER_LANES, ROUTER_LANES), 0) < _iota_f32((ROUTER_LANES, ROUTER_LANES), 1),
                          1.0, 0.0).astype(BF16)
        off_row = MOE_ALIGN * jnp.dot(jnp.broadcast_to(seg_row, (SUBLANES, ROUTER_LANES)).astype(BF16), upper,
                                      preferred_element_type=F32)[0:1, :]
        dest = off_row + rank
        d1c_s[...] = jnp.sum(jnp.where(s1, dest, 0.0), axis=-1, keepdims=True)
        d2c_s[...] = jnp.sum(jnp.where(s2, dest, 0.0), axis=-1, keepdims=True)
        off_i = off_row.astype(jnp.int32)
        cnt_i = cnt_row.astype(jnp.int32)
        for e in range(N_EXPERTS):
            meta[e] = off_i[0, e]
            meta[N_EXPERTS + e] = cnt_i[0, e]

        x = h_ref[...]
        for c in range(pr // MOE_CH):
            prow = _iota_f32((MOE_CH, tb), 0) + float(c * MOE_CH)
            h1 = prow == d1r
            h2 = prow == d2r
            xs[c * MOE_CH:(c + 1) * MOE_CH, :] = jnp.dot(_one_or_zero(h1, h2).astype(BF16), x,
                                                       preferred_element_type=F32).astype(BF16)
            ws[c * MOE_CH:(c + 1) * MOE_CH, :] = jnp.sum(jnp.where(h1, g1r, 0.0) + jnp.where(h2, g2r, 0.0),
                                                       axis=-1, keepdims=True)
        xs[pr:pr + MOE_TR, :] = jnp.zeros((MOE_TR, D_MODEL), BF16)
        ws[pr:pr + MOE_TR, :] = jnp.zeros((MOE_TR, 1), F32)
        out_hi[...] = jnp.zeros(out_hi.shape, BF16)
        out_lo[...] = jnp.zeros(out_lo.shape, BF16)

    def expert_tile(k, r0):
        rows = xs[pl.ds(r0, MOE_TR), :]
        g = jnp.dot(rows, wg_ref[k], preferred_element_type=F32)
        u = jnp.dot(rows, wu_ref[k], preferred_element_type=F32)
        he = (g * jax.nn.sigmoid(g)) * u
        out = jnp.dot(he.astype(BF16), wd_ref[k], preferred_element_type=F32) * ws[pl.ds(r0, MOE_TR), :]
        hi = out.astype(BF16)
        return hi, (out - hi.astype(F32)).astype(BF16)

    def store_tile(r0, hi, lo):
        out_hi[pl.ds(r0, MOE_TR), :] = hi
        out_lo[pl.ds(r0, MOE_TR), :] = lo

    experts = [s * MOE_EXPERTS_PER_STEP + k for k in range(MOE_EXPERTS_PER_STEP)]
    offs = [pl.multiple_of(meta[e], MOE_ALIGN) for e in experts]
    cnts = [meta[N_EXPERTS + e] for e in experts]
    one_tile_each = cnts[0] <= MOE_TR
    for c in cnts[1:]:
        one_tile_each = jnp.logical_and(one_tile_each, c <= MOE_TR)

    @pl.when(one_tile_each)
    def _():
        tiles = [expert_tile(k, offs[k]) for k in range(MOE_EXPERTS_PER_STEP)]
        for k in range(MOE_EXPERTS_PER_STEP):
            store_tile(offs[k], *tiles[k])

    @pl.when(jnp.logical_not(one_tile_each))
    def _():
        for k in range(MOE_EXPERTS_PER_STEP):
            n_tiles = (cnts[k] + (MOE_TR - 1)) // MOE_TR

            @pl.loop(0, n_tiles)
            def _(i):
                r0 = pl.multiple_of(offs[k] + i * MOE_TR, MOE_ALIGN)
                store_tile(r0, *expert_tile(k, r0))

    @pl.when(s == pl.num_programs(1) - 1)
    def _():
        d1c = d1c_s[...]
        d2c = d2c_s[...]
        acc = jnp.zeros((tb, D_MODEL), F32)
        for c in range(pr // MOE_CH):
            pcol = _iota_f32((tb, MOE_CH), 1) + float(c * MOE_CH)
            pt = _one_or_zero(pcol == d1c, pcol == d2c).astype(BF16)
            acc = (acc + jnp.dot(pt, out_hi[c * MOE_CH:(c + 1) * MOE_CH, :], preferred_element_type=F32)
                   + jnp.dot(pt, out_lo[c * MOE_CH:(c + 1) * MOE_CH, :], preferred_element_type=F32))
        o_ref[...] = x1_ref[...] + gt2_ref[...] * acc


def _moe_sorted(h2, route, route_t, x1, gt2, wg_bf, wu_bf, wd_bf):
    rows = h2.shape[0]
    tb = MOE_TB
    pr = -(-(2 * tb + N_EXPERTS * (MOE_ALIGN - 1)) // MOE_CH) * MOE_CH
    eps = MOE_EXPERTS_PER_STEP
    return pl.pallas_call(
        _moe_sorted_kernel,
        grid=(rows // tb, N_EXPERTS // eps),
        in_specs=[pl.BlockSpec((tb, D_MODEL), lambda b, s: (b, 0)),
                  pl.BlockSpec((tb, ROUTER_LANES), lambda b, s: (b, 0)),
                  pl.BlockSpec((SUBLANES, tb), lambda b, s: (0, b)),
                  pl.BlockSpec((tb, D_MODEL), lambda b, s: (b, 0)),
                  pl.BlockSpec((1, D_MODEL), lambda b, s: (0, 0)),
                  pl.BlockSpec((eps, D_MODEL, D_EXPERT), lambda b, s: (s, 0, 0)),
                  pl.BlockSpec((eps, D_MODEL, D_EXPERT), lambda b, s: (s, 0, 0)),
                  pl.BlockSpec((eps, D_EXPERT, D_MODEL), lambda b, s: (s, 0, 0))],
        out_specs=pl.BlockSpec((tb, D_MODEL), lambda b, s: (b, 0)),
        out_shape=jax.ShapeDtypeStruct((rows, D_MODEL), F32),
        scratch_shapes=[pltpu.VMEM((pr + MOE_TR, D_MODEL), BF16), pltpu.VMEM((pr + MOE_TR, D_MODEL), BF16),
                        pltpu.VMEM((pr + MOE_TR, D_MODEL), BF16), pltpu.VMEM((pr + MOE_TR, 1), F32),
                        pltpu.VMEM((tb, 1), F32), pltpu.VMEM((tb, 1), F32),
                        pltpu.SMEM((2 * N_EXPERTS,), jnp.int32)],
        compiler_params=pltpu.CompilerParams(dimension_semantics=("parallel", "arbitrary"),
                                             vmem_limit_bytes=MOE_VMEM_LIMIT),
        name="moe_sorted",
    )(h2, route, route_t, x1, gt2, wg_bf, wu_bf, wd_bf)


def _rope_tables(pos):
    inv = 1.0 / (ROPE_THETA ** (jnp.arange(0, HEAD_DIM, 2, dtype=F32) / HEAD_DIM))
    ang = pos.astype(F32)[:, None] * inv[None, :]
    cos = jnp.cos(ang)
    sin = jnp.sin(ang)
    reps = LANES // (HEAD_DIM // 2)
    cos_t = jnp.tile(cos, (1, reps))
    sin_t = jnp.tile(jnp.concatenate([-sin, sin], axis=-1), (1, reps // 2))
    return cos_t, sin_t


def _lambda_init(layer):
    return 0.8 - 0.6 * math.exp(-0.3 * layer)


def kernel(x_prompt, x_sample, cache_k, cache_v, state_conv, page_table, c_prompt, c_sample, w_ada, b_ada, g_norm1, g_norm2, w_in, g_qnorm, g_knorm, lam_q1, lam_k1, lam_q2, lam_k2, g_subln, w_dw, b_dw, g_ln_conv, b_ln_conv, w_out, w_router_group, b_router_group, w_router_expert, b_router_expert, w_gate_e, w_up_e, w_down_e):
    depth = w_ada.shape[0]
    assert depth == 1, "single-layer trunk"
    layer = 0
    batch_p, seq, _ = x_prompt.shape
    batch_s, n_new, _ = x_sample.shape
    assert batch_p == 1
    past = page_table.shape[1] * PAGE_SIZE
    n_pool = cache_k.shape[1]
    lam_init = _lambda_init(layer)
    row = lambda a: a[layer].reshape(1, -1)

    n_c = batch_p + batch_s
    n_c_pad = -(-n_c // SUBLANES) * SUBLANES
    c_all = jnp.concatenate([c_prompt, c_sample, jnp.zeros((n_c_pad - n_c, D_MODEL), F32)], axis=0)
    mod = _ada(c_all, w_ada[layer], b_ada[layer])
    mod_p = [mod[0:1, i * D_MODEL:(i + 1) * D_MODEL] for i in range(N_MOD)]
    mod_s = [jnp.repeat(mod[1:1 + batch_s, i * D_MODEL:(i + 1) * D_MODEL], n_new, axis=0) for i in range(N_MOD)]

    w_in_bf = w_in[layer].astype(BF16)
    w_out_bf = w_out[layer].astype(BF16)
    wg_bf = w_gate_e[layer].astype(BF16)
    wu_bf = w_up_e[layer].astype(BF16)
    wd_bf = w_down_e[layer].astype(BF16)
    gq = jnp.tile(g_qnorm[layer], QK_WIDTH // HEAD_DIM).reshape(1, -1)
    gk = jnp.tile(g_knorm[layer], QK_WIDTH // HEAD_DIM).reshape(1, -1)
    seg_id = jnp.arange(QK_WIDTH) // HEAD_DIM
    seg = (seg_id[:, None] == seg_id[None, :]).astype(BF16)
    w_r = jnp.concatenate([w_router_expert[layer], w_router_group[layer],
                           jnp.zeros((D_MODEL, ROUTER_LANES - N_EXPERTS - N_GROUPS), F32)], axis=1)
    wr_hi = w_r.astype(BF16)
    wr_lo = (w_r - wr_hi.astype(F32)).astype(BF16)
    b_r = jnp.concatenate([b_router_expert[layer], b_router_group[layer],
                           jnp.zeros((ROUTER_LANES - N_EXPERTS - N_GROUPS,), F32)]).reshape(1, -1)
    lam_p = [row(lam_q1), row(lam_k1), row(lam_q2), row(lam_k2)]
    gs = row(g_subln)
    conv_p = (w_dw[layer], row(b_dw), row(g_ln_conv), row(b_ln_conv))

    xp = x_prompt.reshape(seq, D_MODEL)
    cos_p, sin_p = _rope_tables(jnp.arange(seq))
    k_p, v_p, glu_p, q_bf, k_bf, v_bf = _in_proj(xp, mod_p[0], mod_p[1], row(g_norm1), w_in_bf, gq, gk,
                                                 cos_p, sin_p, seg, tm=512)
    oc_p = _conv_prompt(glu_p, *conv_p, tm=512)
    oa_p = _attn_prompt(q_bf, k_bf, v_bf, lam_p, gs, lam_init, tq=1024, tk=1024)
    x1_p, h2_p, _, route_p = _out_router(xp, oa_p, oc_p, w_out_bf, mod_p[2], mod_p[3], mod_p[4], row(g_norm2),
                                         wr_hi, wr_lo, b_r, tm=512)
    y_p = _moe_sorted(h2_p, route_p, jnp.transpose(route_p[:, :SUBLANES]), x1_p, mod_p[5], wg_bf, wu_bf, wd_bf)

    xs = x_sample.reshape(batch_s * n_new, D_MODEL)
    pos_s = past + (jnp.arange(batch_s * n_new) % n_new)
    cos_s, sin_s = _rope_tables(pos_s)
    rows_s = batch_s * n_new
    k_s, v_s, glu_s, qs_bf, ks_bf, vs_bf = _in_proj(xs, mod_s[0], mod_s[1], row(g_norm1), w_in_bf, gq, gk,
                                                    cos_s, sin_s, seg, tm=rows_s)
    xp_s = jnp.concatenate([state_conv[layer], glu_s.reshape(batch_s, n_new, CONV_CH)], axis=1)
    oc_s = _conv_sample(jnp.transpose(xp_s, (1, 0, 2)), *conv_p, n_new=n_new)
    oc_s = jnp.transpose(oc_s, (1, 0, 2)).reshape(rows_s, CONV_CH)
    tok3 = lambda a: a.reshape(batch_s, n_new, D_ATTN)
    pad_q = lambda a: jnp.pad(tok3(a), ((0, 0), (0, SUBLANES - n_new), (0, 0)))
    oa_s = _attn_sample(page_table, _block_diag_queries(tok3(qs_bf), SUBLANES), pad_q(k_s), pad_q(v_s),
                        jnp.transpose(cache_k[layer], (0, 2, 3, 4, 1)).reshape(n_pool, D_ATTN, PAGE_SIZE),
                        cache_v[layer].reshape(n_pool, PAGE_SIZE * N_HEADS, V_DIM), lam_p, gs, lam_init, n_new)
    oa_s = oa_s[:, :n_new].reshape(rows_s, D_ATTN)
    x1_s, h2_s, gates_s, _ = _out_router(xs, oa_s, oc_s, w_out_bf, mod_s[2], mod_s[3],
                                      mod_s[4], row(g_norm2), wr_hi, wr_lo, b_r, tm=rows_s)
    y_s = _moe(h2_s, gates_s, x1_s, mod_s[5], wg_bf, wu_bf, wd_bf, tm=rows_s)

    hshape = (N_HEADS, 2, HEAD_DIM)
    return (y_p.reshape(batch_p, seq, D_MODEL),
            y_s.reshape(batch_s, n_new, D_MODEL),
            k_p.reshape(depth, batch_p, seq, *hshape),
            v_p.reshape(depth, batch_p, seq, N_HEADS, V_DIM),
            glu_p[seq - (CONV_WIDTH - 1):].reshape(depth, batch_p, CONV_WIDTH - 1, CONV_CH),
            k_s.reshape(depth, batch_s, n_new, *hshape),
            v_s.reshape(depth, batch_s, n_new, N_HEADS, V_DIM),
            xp_s[:, n_new:].reshape(depth, batch_s, CONV_WIDTH - 1, CONV_CH))
```

```python
import functools
import math

import jax
import jax.numpy as jnp
from jax import lax
from jax.experimental import pallas as pl
from jax.experimental.pallas import tpu as pltpu

F32 = jnp.float32
BF16 = jnp.bfloat16

D_MODEL = 1024
HEAD_DIM = 64
V_DIM = 2 * HEAD_DIM
N_HEADS = (D_MODEL // 2) // V_DIM
D_ATTN = N_HEADS * V_DIM
QK_WIDTH = N_HEADS * 2 * HEAD_DIM
CONV_CH = D_MODEL - D_ATTN
CONV_WIDTH = 31
IN_COLS = 2 * QK_WIDTH + D_ATTN + 2 * CONV_CH
N_GROUPS = 4
EXPERTS_PER_GROUP = 8
N_EXPERTS = N_GROUPS * EXPERTS_PER_GROUP
D_EXPERT = D_MODEL // 4
ROPE_THETA = 10000.0
PAGE_SIZE = 128
EPS = 1e-6
N_MOD = 6
NEG_INF = -1e30
QK_SCALE_LOG2 = HEAD_DIM ** -0.5 * math.log2(math.e)

LANES = 128
SUBLANES = 8
VMEM_LIMIT = 48 * 1024 * 1024
HALO = 32
CONV_CHUNK = 64
PAGES_PER_STEP = 32
ROUTER_LANES = 128
MOE_TB = 1024
MOE_TR = 256
MOE_CH = 512
MOE_ALIGN = 16
MOE_EXPERTS_PER_STEP = 2
MOE_VMEM_LIMIT = 56 * 1024 * 1024


def _params(*sem):
    return pltpu.CompilerParams(dimension_semantics=sem, vmem_limit_bytes=VMEM_LIMIT)


def _nt_dot(a, b):
    return lax.dot_general(a, b, (((1,), (1,)), ((), ())), preferred_element_type=F32)


def _ada_kernel(c_ref, w_ref, b_ref, o_ref):
    c = c_ref[...]
    a = (c * jax.nn.sigmoid(c)).astype(BF16)
    o_ref[...] = jnp.dot(a, w_ref[...].astype(BF16), preferred_element_type=F32) + b_ref[...]


def _ada(c_all, w_ada, b_ada):
    rows = c_all.shape[0]
    return pl.pallas_call(
        _ada_kernel,
        grid=(N_MOD,),
        in_specs=[
            pl.BlockSpec((rows, D_MODEL), lambda j: (0, 0)),
            pl.BlockSpec((D_MODEL, D_MODEL), lambda j: (0, j)),
            pl.BlockSpec((1, D_MODEL), lambda j: (0, j)),
        ],
        out_specs=pl.BlockSpec((rows, D_MODEL), lambda j: (0, j)),
        out_shape=jax.ShapeDtypeStruct((rows, N_MOD * D_MODEL), F32),
        compiler_params=_params("arbitrary"),
        name="ada",
    )(c_all, w_ada, b_ada.reshape(1, -1))


def _rms_rows(x, g):
    return x * lax.rsqrt(jnp.mean(x * x, axis=-1, keepdims=True) + EPS) * g


def _qk_norm_rope(t, g, seg, cos, sin_signed, first_half):
    ms = jnp.dot((t * t).astype(BF16), seg, preferred_element_type=F32) * (1.0 / HEAD_DIM)
    tn = t * lax.rsqrt(ms + EPS) * g
    out = []
    for i in range(t.shape[1] // LANES):
        xs = tn[:, i * LANES:(i + 1) * LANES]
        rot = jnp.where(first_half, pltpu.roll(xs, LANES - HEAD_DIM // 2, 1), pltpu.roll(xs, HEAD_DIM // 2, 1))
        out.append(xs * cos + rot * sin_signed)
    return jnp.concatenate(out, axis=1)


def _in_proj_kernel(x_ref, sh_ref, sc_ref, g1_ref, w_ref, gq_ref, gk_ref, cos_ref, sin_ref, seg_ref,
                    k_out, v_out, glu_out, q_bf, k_bf, v_bf):
    x = x_ref[...]
    h = _rms_rows(x, g1_ref[...]) * (1.0 + sc_ref[...]) + sh_ref[...]
    proj = jnp.dot(h.astype(BF16), w_ref[...], preferred_element_type=F32)
    cos = cos_ref[...]
    sin = sin_ref[...]
    seg = seg_ref[...]
    lane = lax.broadcasted_iota(jnp.int32, cos.shape, 1)
    first_half = (lane % HEAD_DIM) < (HEAD_DIM // 2)
    q = _qk_norm_rope(proj[:, :QK_WIDTH], gq_ref[...], seg, cos, sin, first_half)
    k = _qk_norm_rope(proj[:, QK_WIDTH:2 * QK_WIDTH], gk_ref[...], seg, cos, sin, first_half)
    v = proj[:, 2 * QK_WIDTH:2 * QK_WIDTH + D_ATTN]
    a = proj[:, 2 * QK_WIDTH + D_ATTN:2 * QK_WIDTH + D_ATTN + CONV_CH]
    gate = proj[:, 2 * QK_WIDTH + D_ATTN + CONV_CH:]
    k_out[...] = k
    for hd in range(N_HEADS):
        v_out[pl.ds(hd, v.shape[0], stride=N_HEADS), :] = v[:, hd * V_DIM:(hd + 1) * V_DIM]
    glu_out[...] = a * jax.nn.sigmoid(gate)
    q_bf[...] = (q * QK_SCALE_LOG2).astype(BF16)
    k_bf[...] = k.astype(BF16)
    v_bf[...] = v.astype(BF16)


def _in_proj(x, sh, sc, g1, w_in_bf, gq, gk, cos, sin, seg, tm):
    rows = x.shape[0]
    per_row = sh.shape[0] != 1
    mod_spec = pl.BlockSpec((tm, D_MODEL), lambda i: (i, 0)) if per_row else pl.BlockSpec((1, D_MODEL), lambda i: (0, 0))
    const = lambda shape: pl.BlockSpec(shape, lambda i: (0, 0))
    row_spec = lambda width: pl.BlockSpec((tm, width), lambda i: (i, 0))
    f32_out = jax.ShapeDtypeStruct((rows, QK_WIDTH), F32)
    bf_out = jax.ShapeDtypeStruct((rows, QK_WIDTH), BF16)
    return pl.pallas_call(
        _in_proj_kernel,
        grid=(rows // tm,),
        in_specs=[row_spec(D_MODEL), mod_spec, mod_spec, const((1, D_MODEL)), const((D_MODEL, IN_COLS)),
                  const((1, QK_WIDTH)), const((1, QK_WIDTH)), row_spec(LANES), row_spec(LANES),
                  const((QK_WIDTH, QK_WIDTH))],
        out_specs=[row_spec(QK_WIDTH), pl.BlockSpec((tm * N_HEADS, V_DIM), lambda i: (i, 0))] + [row_spec(QK_WIDTH)] * 4,
        out_shape=[f32_out, jax.ShapeDtypeStruct((rows * N_HEADS, V_DIM), F32), f32_out, bf_out, bf_out, bf_out],
        compiler_params=_params("parallel"),
        name="in_proj",
    )(x, sh, sc, g1, w_in_bf, gq, gk, cos, sin, seg)


def _ln_swish(y, g, b):
    mu = jnp.mean(y, axis=-1, keepdims=True)
    d = y - mu
    var = jnp.mean(d * d, axis=-1, keepdims=True)
    z = d * lax.rsqrt(var + EPS) * g + b
    return z * jax.nn.sigmoid(z)


def _conv_prompt_kernel(glu_ref, w_ref, bdw_ref, g_ref, b_ref, o_ref, xpad, shifted):
    tm = glu_ref.shape[0]

    @pl.when(pl.program_id(0) == 0)
    def _():
        xpad[0:HALO, :] = jnp.zeros((HALO, CONV_CH), F32)

    xpad[HALO:HALO + tm, :] = glu_ref[...]
    base = HALO - (CONV_WIDTH - 1)
    for c in range(tm // CONV_CHUNK):
        r0 = c * CONV_CHUNK
        acc = jnp.zeros((CONV_CHUNK, CONV_CH), F32)
        for phase in range(SUBLANES):
            taps = [j for j in range(CONV_WIDTH) if (base + j) % SUBLANES == phase]
            reach = max(base + j - phase for j in taps) + CONV_CHUNK
            shifted[phase, 0:reach, :] = xpad[r0 + phase:r0 + phase + reach, :]
            for j in taps:
                a = base + j - phase
                acc = acc + shifted[phase, a:a + CONV_CHUNK, :] * w_ref[j:j + 1, :]
        y = acc + bdw_ref[...]
        o_ref[r0:r0 + CONV_CHUNK, :] = _ln_swish(y, g_ref[...], b_ref[...]).astype(BF16)
    xpad[0:HALO, :] = xpad[tm:tm + HALO, :]


def _conv_prompt(glu, w_dw, b_dw, g_ln, b_ln, tm):
    rows = glu.shape[0]
    const = lambda shape: pl.BlockSpec(shape, lambda i: (0, 0))
    return pl.pallas_call(
        _conv_prompt_kernel,
        grid=(rows // tm,),
        in_specs=[pl.BlockSpec((tm, CONV_CH), lambda i: (i, 0)), const((CONV_WIDTH, CONV_CH)),
                  const((1, CONV_CH)), const((1, CONV_CH)), const((1, CONV_CH))],
        out_specs=pl.BlockSpec((tm, CONV_CH), lambda i: (i, 0)),
        out_shape=jax.ShapeDtypeStruct((rows, CONV_CH), BF16),
        scratch_shapes=[pltpu.VMEM((tm + HALO, CONV_CH), F32),
                        pltpu.VMEM((SUBLANES, CONV_CHUNK + HALO, CONV_CH), F32)],
        compiler_params=_params("arbitrary"),
        name="conv_prompt",
    )(glu, w_dw, b_dw, g_ln, b_ln)


def _conv_sample_kernel(xp_ref, w_ref, bdw_ref, g_ref, b_ref, o_ref):
    n_new = o_ref.shape[0]
    for i in range(n_new):
        acc = jnp.zeros(o_ref.shape[1:], F32)
        for j in range(CONV_WIDTH):
            acc = acc + xp_ref[i + j] * w_ref[j:j + 1, :]
        o_ref[i] = _ln_swish(acc + bdw_ref[...], g_ref[...], b_ref[...]).astype(BF16)


def _conv_sample(xp_t, w_dw, b_dw, g_ln, b_ln, n_new):
    t, batch, ch = xp_t.shape
    const2 = lambda shape: pl.BlockSpec(shape, lambda i: (0, 0))
    return pl.pallas_call(
        _conv_sample_kernel,
        grid=(1,),
        in_specs=[pl.BlockSpec((t, batch, ch), lambda i: (0, 0, 0)), const2((CONV_WIDTH, ch)),
                  const2((1, ch)), const2((1, ch)), const2((1, ch))],
        out_specs=pl.BlockSpec((n_new, batch, ch), lambda i: (0, 0, 0)),
        out_shape=jax.ShapeDtypeStruct((n_new, batch, ch), BF16),
        compiler_params=_params("arbitrary"),
        name="conv_sample",
    )(xp_t, w_dw, b_dw, g_ln, b_ln)


def _lambda(lq1, lk1, lq2, lk2, lam_init):
    s1 = jnp.sum(lq1[...] * lk1[...], axis=-1, keepdims=True)
    s2 = jnp.sum(lq2[...] * lk2[...], axis=-1, keepdims=True)
    return jnp.exp(s1) - jnp.exp(s2) + lam_init


def _online_step(s, vb, m, l, acc):
    m_new = jnp.maximum(m, jnp.max(s, axis=-1, keepdims=True))
    alpha = jnp.exp2(m - m_new)
    e = jnp.exp2(s - m_new)
    l = alpha * l + jnp.sum(e, axis=-1, keepdims=True)
    acc = alpha * acc + jnp.dot(e.astype(BF16), vb, preferred_element_type=F32)
    return m_new, l, acc


def _attn_prompt_kernel(q_ref, k_ref, v_ref, lq1, lk1, lq2, lk2, gs_ref, o_ref, m_scr, l_scr, acc_scr, lp_scr,
                        kmax_scr, *, lam_init, tk):
    tq = q_ref.shape[0]
    qi = pl.program_id(1)
    q = q_ref[...]
    lane = lax.broadcasted_iota(jnp.int32, q.shape, 1)
    zero = jnp.zeros_like(q)
    q2 = jnp.concatenate([jnp.where(lane < HEAD_DIM, q, zero), jnp.where(lane >= HEAD_DIM, q, zero)], axis=0)

    @pl.when(qi == 0)
    def _():
        def chunk_max(c, carry):
            kb = k_ref[pl.ds(pl.multiple_of(c * tk, tk), tk), :].astype(F32)
            kk = kb * kb
            lane_k = lax.broadcasted_iota(jnp.int32, kk.shape, 1)
            n0 = jnp.sum(jnp.where(lane_k < HEAD_DIM, kk, 0.0), axis=-1, keepdims=True)
            n1 = jnp.sum(jnp.where(lane_k >= HEAD_DIM, kk, 0.0), axis=-1, keepdims=True)
            return (jnp.maximum(carry[0], jnp.max(n0, axis=0, keepdims=True)),
                    jnp.maximum(carry[1], jnp.max(n1, axis=0, keepdims=True)))

        k0, k1 = lax.fori_loop(0, k_ref.shape[0] // tk, chunk_max, (jnp.zeros((1, 1), F32), jnp.zeros((1, 1), F32)))
        kmax_scr[0:1, :] = jnp.broadcast_to(k0, (1, LANES))
        kmax_scr[1:2, :] = jnp.broadcast_to(k1, (1, LANES))

    qf = q2.astype(F32)
    kmax = jnp.concatenate([jnp.broadcast_to(kmax_scr[0:1, 0:1], (tq, 1)),
                            jnp.broadcast_to(kmax_scr[1:2, 0:1], (tq, 1))], axis=0)
    bound = jnp.sqrt(jnp.sum(qf * qf, axis=-1, keepdims=True) * kmax)

    def scores(j, masked):
        off = pl.multiple_of(j * tk, tk)
        s = _nt_dot(q2, k_ref[pl.ds(off, tk), :])
        if masked:
            row = lax.broadcasted_iota(jnp.int32, s.shape, 0) % tq + qi * tq
            col = lax.broadcasted_iota(jnp.int32, s.shape, 1) + off
            s = jnp.where(col <= row, s, NEG_INF)
        return s, v_ref[pl.ds(off, tk), :]

    def fold_lanes(e):
        part = e[:, 0:LANES]
        for t in range(1, e.shape[1] // LANES):
            part = part + e[:, t * LANES:(t + 1) * LANES]
        return part

    def fast(j, masked):
        if masked and tq == tk:
            return fast_diagonal(j)
        s, vb = scores(j, masked)
        e = jnp.exp2(s - bound)
        lp_scr[...] += fold_lanes(e)
        acc_scr[...] += jnp.dot(e.astype(BF16), vb, preferred_element_type=F32)

    def fast_diagonal(j):
        hq = tq // 2
        off = pl.multiple_of(j * tk, tk)
        s = _nt_dot(q2, k_ref[pl.ds(off, hq), :])
        row = lax.broadcasted_iota(jnp.int32, s.shape, 0) % tq
        col = lax.broadcasted_iota(jnp.int32, s.shape, 1)
        e = jnp.exp2(jnp.where(col <= row, s, NEG_INF) - bound)
        lp_scr[...] += fold_lanes(e)
        acc_scr[...] += jnp.dot(e.astype(BF16), v_ref[pl.ds(off, hq), :], preferred_element_type=F32)

        late = [slice(hq, tq), slice(tq + hq, 2 * tq)]
        q_late = jnp.concatenate([q2[r] for r in late], axis=0)
        b_late = jnp.concatenate([bound[r] for r in late], axis=0)
        s = _nt_dot(q_late, k_ref[pl.ds(pl.multiple_of(off + hq, hq), hq), :])
        row = lax.broadcasted_iota(jnp.int32, s.shape, 0) % hq
        col = lax.broadcasted_iota(jnp.int32, s.shape, 1)
        e = jnp.exp2(jnp.where(col <= row, s, NEG_INF) - b_late)
        part = fold_lanes(e)
        pv = jnp.dot(e.astype(BF16), v_ref[pl.ds(pl.multiple_of(off + hq, hq), hq), :], preferred_element_type=F32)
        for n, r in enumerate(late):
            lp_scr[r, :] += part[n * hq:(n + 1) * hq]
            acc_scr[r, :] += pv[n * hq:(n + 1) * hq]

    def exact(j, masked):
        s, vb = scores(j, masked)
        m, l, acc = _online_step(s, vb, m_scr[...], l_scr[...], acc_scr[...])
        m_scr[...] = m
        l_scr[...] = l
        acc_scr[...] = acc

    def finish(l):
        lam = _lambda(lq1, lk1, lq2, lk2, lam_init)
        acc = acc_scr[...]
        o = acc[:tq] / l[:tq] - lam * (acc[tq:] / l[tq:])
        o_ref[...] = (_rms_rows(o, gs_ref[...]) * (1.0 - lam_init)).astype(BF16)

    n_full = (qi * tq) // tk
    n_pairs = n_full // 2
    last = 2 * n_pairs
    lp_scr[...] = jnp.zeros(lp_scr.shape, F32)
    acc_scr[...] = jnp.zeros(acc_scr.shape, F32)

    @pl.loop(0, n_pairs)
    def _(p):
        fast(2 * p, False)
        fast(2 * p + 1, False)

    @pl.when(n_full == last)
    def _():
        fast(last, True)

    @pl.when(n_full != last)
    def _():
        fast(last, False)
        fast(last + 1, True)

    l_fast = jnp.sum(lp_scr[...], axis=-1, keepdims=True)
    trusted = jnp.logical_and(jnp.min(l_fast) > 2.0 ** -60, jnp.max(l_fast) < 2.0 ** 100)

    @pl.when(trusted)
    def _():
        finish(l_fast)

    @pl.when(jnp.logical_not(trusted))
    def _():
        m_scr[...] = jnp.full(m_scr.shape, NEG_INF, F32)
        l_scr[...] = jnp.zeros(l_scr.shape, F32)
        acc_scr[...] = jnp.zeros(acc_scr.shape, F32)

        @pl.loop(0, n_full)
        def _(j):
            exact(j, False)

        exact(n_full, True)
        finish(l_scr[...])


def _attn_prompt(q_bf, k_bf, v_bf, lam_p, g_subln, lam_init, tq, tk):
    s = q_bf.shape[0]
    assert tk % tq == 0 and s % tk == 0
    vec = lambda n: pl.BlockSpec((1, n), lambda h, i: (0, 0))
    return pl.pallas_call(
        functools.partial(_attn_prompt_kernel, lam_init=lam_init, tk=tk),
        grid=(N_HEADS, s // tq),
        in_specs=[pl.BlockSpec((tq, V_DIM), lambda h, i: (i, h)),
                  pl.BlockSpec((s, V_DIM), lambda h, i: (0, h)),
                  pl.BlockSpec((s, V_DIM), lambda h, i: (0, h)),
                  vec(HEAD_DIM), vec(HEAD_DIM), vec(HEAD_DIM), vec(HEAD_DIM), vec(V_DIM)],
        out_specs=pl.BlockSpec((tq, V_DIM), lambda h, i: (i, h)),
        out_shape=jax.ShapeDtypeStruct((s, D_ATTN), BF16),
        scratch_shapes=[pltpu.VMEM((2 * tq, 1), F32), pltpu.VMEM((2 * tq, 1), F32),
                        pltpu.VMEM((2 * tq, V_DIM), F32), pltpu.VMEM((2 * tq, LANES), F32),
                        pltpu.VMEM((SUBLANES, LANES), F32)],
        compiler_params=_params("arbitrary", "arbitrary"),
        name="attn_prompt",
    )(q_bf, k_bf, v_bf, *lam_p, g_subln)


def _attn_sample_kernel(pt_ref, qbd_ref, kn_ref, vn_ref, *rest, lam_init, n_new):
    del pt_ref
    npg = PAGES_PER_STEP
    kp = rest[:npg]
    vp = rest[npg:2 * npg]
    lq1, lk1, lq2, lk2, gs_ref, o_ref, m_s, l_s, acc_s = rest[2 * npg:]
    j = pl.program_id(1)
    n_rows = qbd_ref.shape[1]
    hrows = n_rows // N_HEADS
    qpad = hrows // 2

    @pl.when(j == 0)
    def _():
        m_s[...] = jnp.full(m_s.shape, NEG_INF, F32)
        l_s[...] = jnp.zeros(l_s.shape, F32)
        acc_s[...] = jnp.zeros(acc_s.shape, F32)

    qbd = qbd_ref[0]
    kc = jnp.concatenate([r[0] for r in kp], axis=1).astype(BF16)
    s = jnp.dot(qbd, kc, preferred_element_type=F32)
    m = m_s[...]
    m_new = jnp.maximum(m, jnp.max(s, axis=-1, keepdims=True))
    alpha = jnp.exp2(m - m_new)
    e = jnp.exp2(s - m_new)
    l = alpha * l_s[...] + jnp.sum(e, axis=-1, keepdims=True)
    e = e.astype(BF16)
    pv = []
    for h in range(N_HEADS):
        vh = jnp.concatenate([r[0, pl.ds(h, PAGE_SIZE, stride=N_HEADS), :] for r in vp], axis=0).astype(BF16)
        pv.append(jnp.dot(e[h * hrows:(h + 1) * hrows], vh, preferred_element_type=F32))
    acc = alpha * acc_s[...] + jnp.concatenate(pv, axis=0)
    m_s[...] = m_new
    l_s[...] = l
    acc_s[...] = acc

    @pl.when(j == pl.num_programs(1) - 1)
    def _():
        qf = qbd.astype(F32)
        kn = kn_ref[0].astype(BF16).astype(F32)
        vn = vn_ref[0].astype(BF16).astype(F32)
        qpos = lax.broadcasted_iota(jnp.int32, (n_rows, 1), 0) % qpad
        s_new = [jnp.where(t <= qpos, jnp.sum(qf * kn[t:t + 1, :], axis=-1, keepdims=True), NEG_INF)
                 for t in range(n_new)]
        m_f = m_new
        for s_t in s_new:
            m_f = jnp.maximum(m_f, s_t)
        a_f = jnp.exp2(m_new - m_f)
        l_f = a_f * l
        acc_f = a_f * acc
        e_new = [jnp.exp2(s_t - m_f) for s_t in s_new]
        for e_t in e_new:
            l_f = l_f + e_t
        lam = _lambda(lq1, lk1, lq2, lk2, lam_init)
        gs = gs_ref[...]
        outs = []
        for h in range(N_HEADS):
            a_h = acc_f[h * hrows:(h + 1) * hrows]
            for t in range(n_new):
                e_t = e_new[t][h * hrows:(h + 1) * hrows].astype(BF16).astype(F32)
                a_h = a_h + e_t * vn[t:t + 1, h * V_DIM:(h + 1) * V_DIM]
            o_h = a_h / l_f[h * hrows:(h + 1) * hrows]
            outs.append(_rms_rows(o_h[:qpad] - lam * o_h[qpad:], gs))
        o_ref[0] = (jnp.concatenate(outs, axis=1) * (1.0 - lam_init)).astype(BF16)


def _attn_sample(page_table, qbd, kn_pad, vn_pad, cache_kt, cache_v, lam_p, g_subln, lam_init, n_new):
    batch, n_rows, _ = qbd.shape
    qpad = kn_pad.shape[1]
    n_pages = page_table.shape[1]
    steps = n_pages // PAGES_PER_STEP
    pt_flat = page_table.reshape(-1)
    tok = lambda r: pl.BlockSpec((1, r, D_ATTN), lambda b, j, pt: (b, 0, 0))

    def page_spec(p, shape):
        return pl.BlockSpec((1,) + shape, lambda b, j, pt: (pt[b * n_pages + j * PAGES_PER_STEP + p], 0, 0))

    vec = lambda n: pl.BlockSpec((1, n), lambda b, j, pt: (0, 0))
    k_pages = [page_spec(p, (D_ATTN, PAGE_SIZE)) for p in range(PAGES_PER_STEP)]
    v_pages = [page_spec(p, (PAGE_SIZE * N_HEADS, V_DIM)) for p in range(PAGES_PER_STEP)]
    grid_spec = pltpu.PrefetchScalarGridSpec(
        num_scalar_prefetch=1,
        grid=(batch, steps),
        in_specs=[tok(n_rows), tok(qpad), tok(qpad)] + k_pages + v_pages
                 + [vec(HEAD_DIM), vec(HEAD_DIM), vec(HEAD_DIM), vec(HEAD_DIM), vec(V_DIM)],
        out_specs=tok(qpad),
        scratch_shapes=[pltpu.VMEM((n_rows, 1), F32), pltpu.VMEM((n_rows, 1), F32),
                        pltpu.VMEM((n_rows, V_DIM), F32)],
    )
    return pl.pallas_call(
        functools.partial(_attn_sample_kernel, lam_init=lam_init, n_new=n_new),
        grid_spec=grid_spec,
        out_shape=jax.ShapeDtypeStruct((batch, qpad, D_ATTN), BF16),
        compiler_params=_params("parallel", "arbitrary"),
        name="attn_sample",
    )(pt_flat, qbd, kn_pad, vn_pad, *([cache_kt] * PAGES_PER_STEP), *([cache_v] * PAGES_PER_STEP),
      *lam_p, g_subln)


def _block_diag_queries(q_bf, qpad):
    batch, n_new, _ = q_bf.shape
    qp = jnp.pad(q_bf, ((0, 0), (0, qpad - n_new), (0, 0)))
    qt = jnp.tile(qp, (1, 2 * N_HEADS, 1))
    chunk = jnp.arange(2 * N_HEADS * qpad) // qpad
    keep = (jnp.arange(D_ATTN)[None, :] // HEAD_DIM) == chunk[:, None]
    return jnp.where(keep[None], qt, jnp.zeros_like(qt))


def _lane_min_where(cond, lane, width):
    return jnp.min(jnp.where(cond, lane, width), axis=-1, keepdims=True)


def _out_router_kernel(x_ref, oa_ref, oc_ref, w_ref, gt1_ref, sh2_ref, sc2_ref, g2_ref,
                       wr_hi_ref, wr_lo_ref, br_ref, x1_out, h2_out, gates_out, route_out):
    mix = (jnp.dot(oa_ref[...], w_ref[0:D_ATTN, :], preferred_element_type=F32)
           + jnp.dot(oc_ref[...], w_ref[D_ATTN:, :], preferred_element_type=F32))
    x1 = x_ref[...] + gt1_ref[...] * mix
    x1_out[...] = x1
    h2 = _rms_rows(x1, g2_ref[...]) * (1.0 + sc2_ref[...]) + sh2_ref[...]
    h2_out[...] = h2.astype(BF16)

    h_hi = h2.astype(BF16)
    h_lo = (h2 - h_hi.astype(F32)).astype(BF16)
    logits = (jnp.dot(h_hi, wr_hi_ref[...], preferred_element_type=F32)
              + jnp.dot(h_hi, wr_lo_ref[...], preferred_element_type=F32)
              + jnp.dot(h_lo, wr_hi_ref[...], preferred_element_type=F32)) + br_ref[...]
    lane_i = lax.broadcasted_iota(jnp.int32, logits.shape, 1)
    lane = lane_i.astype(F32)
    lg = jnp.where(lane_i >= N_EXPERTS, jnp.where(lane_i < N_EXPERTS + N_GROUPS, logits, NEG_INF), NEG_INF)
    mg = jnp.max(lg, axis=-1, keepdims=True)
    pg_top = 1.0 / jnp.sum(jnp.exp(lg - mg), axis=-1, keepdims=True)
    g_idx = _lane_min_where(lg == mg, lane, float(ROUTER_LANES)) - N_EXPERTS
    lane_group = (lane_i // EXPERTS_PER_GROUP).astype(F32)
    le = jnp.where(lane_i < N_EXPERTS, jnp.where(lane_group == g_idx, logits, NEG_INF), NEG_INF)
    m1 = jnp.max(le, axis=-1, keepdims=True)
    i1 = _lane_min_where(le == m1, lane, float(ROUTER_LANES))
    le2 = jnp.where(lane == i1, NEG_INF, le)
    m2 = jnp.max(le2, axis=-1, keepdims=True)
    i2 = _lane_min_where(le2 == m2, lane, float(ROUTER_LANES))
    r = jnp.exp(m2 - m1)
    w1 = 1.0 / (1.0 + r)
    w2 = r / (1.0 + r)
    g1 = pg_top * w1
    g2 = pg_top * w2
    gates_out[...] = jnp.where(lane == i1, g1, 0.0) + jnp.where(lane == i2, g2, 0.0)
    route_out[...] = jnp.where(lane_i == 0, i1, jnp.where(lane_i == 1, i2, jnp.where(lane_i == 2, g1,
                               jnp.where(lane_i == 3, g2, 0.0))))


def _out_router(x, oa, oc, w_out_bf, gt1, sh2, sc2, g2, wr_hi, wr_lo, br, tm):
    rows = x.shape[0]
    per_row = gt1.shape[0] != 1
    mod_spec = pl.BlockSpec((tm, D_MODEL), lambda i: (i, 0)) if per_row else pl.BlockSpec((1, D_MODEL), lambda i: (0, 0))
    const = lambda shape: pl.BlockSpec(shape, lambda i: (0, 0))
    row_spec = lambda width: pl.BlockSpec((tm, width), lambda i: (i, 0))
    return pl.pallas_call(
        _out_router_kernel,
        grid=(rows // tm,),
        in_specs=[row_spec(D_MODEL), row_spec(D_ATTN), row_spec(CONV_CH), const((D_MODEL, D_MODEL)),
                  mod_spec, mod_spec, mod_spec, const((1, D_MODEL)),
                  const((D_MODEL, ROUTER_LANES)), const((D_MODEL, ROUTER_LANES)), const((1, ROUTER_LANES))],
        out_specs=[row_spec(D_MODEL), row_spec(D_MODEL), row_spec(ROUTER_LANES), row_spec(ROUTER_LANES)],
        out_shape=[jax.ShapeDtypeStruct((rows, D_MODEL), F32), jax.ShapeDtypeStruct((rows, D_MODEL), BF16),
                   jax.ShapeDtypeStruct((rows, ROUTER_LANES), F32), jax.ShapeDtypeStruct((rows, ROUTER_LANES), F32)],
        compiler_params=_params("parallel"),
        name="out_router",
    )(x, oa, oc, w_out_bf, gt1, sh2, sc2, g2, wr_hi, wr_lo, br)


def _moe_kernel(h_ref, gates_ref, x1_ref, gt2_ref, wg_ref, wu_ref, wd_ref, y_ref, acc):
    e = pl.program_id(1)

    @pl.when(e == 0)
    def _():
        acc[...] = jnp.zeros(acc.shape, F32)

    t = h_ref[...]
    g = jnp.dot(t, wg_ref[0], preferred_element_type=F32)
    u = jnp.dot(t, wu_ref[0], preferred_element_type=F32)
    he = (g * jax.nn.sigmoid(g)) * u
    out = jnp.dot(he.astype(BF16), wd_ref[0], preferred_element_type=F32)
    gates = gates_ref[...]
    lane = lax.broadcasted_iota(jnp.int32, gates.shape, 1)
    ge = jnp.sum(jnp.where(lane == e, gates, 0.0), axis=-1, keepdims=True)
    acc[...] += ge * out

    @pl.when(e == pl.num_programs(1) - 1)
    def _():
        y_ref[...] = x1_ref[...] + gt2_ref[...] * acc[...]


def _moe(h2, gates, x1, gt2, wg_bf, wu_bf, wd_bf, tm):
    rows = h2.shape[0]
    per_row = gt2.shape[0] != 1
    mod_spec = (pl.BlockSpec((tm, D_MODEL), lambda i, e: (i, 0)) if per_row
                else pl.BlockSpec((1, D_MODEL), lambda i, e: (0, 0)))
    row_spec = lambda width: pl.BlockSpec((tm, width), lambda i, e: (i, 0))
    return pl.pallas_call(
        _moe_kernel,
        grid=(rows // tm, N_EXPERTS),
        in_specs=[row_spec(D_MODEL), row_spec(ROUTER_LANES), row_spec(D_MODEL), mod_spec,
                  pl.BlockSpec((1, D_MODEL, D_EXPERT), lambda i, e: (e, 0, 0)),
                  pl.BlockSpec((1, D_MODEL, D_EXPERT), lambda i, e: (e, 0, 0)),
                  pl.BlockSpec((1, D_EXPERT, D_MODEL), lambda i, e: (e, 0, 0))],
        out_specs=row_spec(D_MODEL),
        out_shape=jax.ShapeDtypeStruct((rows, D_MODEL), F32),
        scratch_shapes=[pltpu.VMEM((tm, D_MODEL), F32)],
        compiler_params=_params("parallel", "arbitrary"),
        name="moe",
    )(h2, gates, x1, gt2, wg_bf, wu_bf, wd_bf)


def _iota_f32(shape, dim):
    return lax.broadcasted_iota(jnp.int32, shape, dim).astype(F32)


def _one_or_zero(a, b):
    return jnp.where(a, 1.0, jnp.where(b, 1.0, 0.0))


def _moe_sorted_kernel(h_ref, route_ref, route_t_ref, x1_ref, gt2_ref, wg_ref, wu_ref, wd_ref, o_ref,
                       xs, out_hi, out_lo, ws, d1c_s, d2c_s, meta):
    s = pl.program_id(1)
    tb = h_ref.shape[0]
    pr = xs.shape[0] - MOE_TR

    @pl.when(s == 0)
    def _():
        rt = route_t_ref[...]
        i1r, i2r, g1r, g2r = rt[0:1], rt[1:2], rt[2:3], rt[3:4]
        eio = _iota_f32((ROUTER_LANES, tb), 0)
        s1t = eio == i1r
        s2t = eio == i2r
        mt = _one_or_zero(s1t, s2t).astype(BF16)
        ranks = []
        for c in range(tb // MOE_TR):
            before = _iota_f32((tb, MOE_TR), 0) < _iota_f32((tb, MOE_TR), 1) + float(c * MOE_TR)
            ranks.append(jnp.dot(mt, jnp.where(before, 1.0, 0.0).astype(BF16), preferred_element_type=F32))
        rank_t = jnp.concatenate(ranks, axis=1)
        cnt_col = jnp.sum(mt.astype(F32), axis=1, keepdims=True)
        seg_col = jnp.floor((cnt_col + (MOE_ALIGN - 1.0)) * (1.0 / MOE_ALIGN))
        lower = jnp.where(_iota_f32((ROUTER_LANES, ROUTER_LANES), 1) < _iota_f32((ROUTER_LANES, ROUTER_LANES), 0),
                          1.0, 0.0).astype(BF16)
        off_col = MOE_ALIGN * jnp.dot(lower, jnp.broadcast_to(seg_col, (ROUTER_LANES, ROUTER_LANES)).astype(BF16),
                                      preferred_element_type=F32)[:, 0:1]
        dest_t = off_col + rank_t
        d1r = jnp.sum(jnp.where(s1t, dest_t, 0.0), axis=0, keepdims=True)
        d2r = jnp.sum(jnp.where(s2t, dest_t, 0.0), axis=0, keepdims=True)

        r = route_ref[...]
        lio = _iota_f32((tb, ROUTER_LANES), 1)
        s1 = lio == r[:, 0:1]
        s2 = lio == r[:, 1:2]
        m = _one_or_zero(s1, s2).astype(BF16)
        ranks = []
        for c in range(tb // MOE_TR):
            before = _iota_f32((MOE_TR, tb), 1) < _iota_f32((MOE_TR, tb), 0) + float(c * MOE_TR)
            ranks.append(jnp.dot(jnp.where(before, 1.0, 0.0).astype(BF16), m, preferred_element_type=F32))
        rank = jnp.concatenate(ranks, axis=0)
        cnt_row = jnp.sum(m.astype(F32), axis=0, keepdims=True)
        seg_row = jnp.floor((cnt_row + (MOE_ALIGN - 1.0)) * (1.0 / MOE_ALIGN))
        upper = jnp.where(_iota_f32((ROUTER_LANES, ROUTER_LANES), 0) < _iota_f32((ROUTER_LANES, ROUTER_LANES), 1),
                          1.0, 0.0).astype(BF16)
        off_row = MOE_ALIGN * jnp.dot(jnp.broadcast_to(seg_row, (SUBLANES, ROUTER_LANES)).astype(BF16), upper,
                                      preferred_element_type=F32)[0:1, :]
        dest = off_row + rank
        d1c_s[...] = jnp.sum(jnp.where(s1, dest, 0.0), axis=-1, keepdims=True)
        d2c_s[...] = jnp.sum(jnp.where(s2, dest, 0.0), axis=-1, keepdims=True)
        off_i = off_row.astype(jnp.int32)
        cnt_i = cnt_row.astype(jnp.int32)
        for e in range(N_EXPERTS):
            meta[e] = off_i[0, e]
            meta[N_EXPERTS + e] = cnt_i[0, e]

        x = h_ref[...]
        for c in range(pr // MOE_CH):
            prow = _iota_f32((MOE_CH, tb), 0) + float(c * MOE_CH)
            h1 = prow == d1r
            h2 = prow == d2r
            xs[c * MOE_CH:(c + 1) * MOE_CH, :] = jnp.dot(_one_or_zero(h1, h2).astype(BF16), x,
                                                       preferred_element_type=F32).astype(BF16)
            ws[c * MOE_CH:(c + 1) * MOE_CH, :] = jnp.sum(jnp.where(h1, g1r, 0.0) + jnp.where(h2, g2r, 0.0),
                                                       axis=-1, keepdims=True)
        xs[pr:pr + MOE_TR, :] = jnp.zeros((MOE_TR, D_MODEL), BF16)
        ws[pr:pr + MOE_TR, :] = jnp.zeros((MOE_TR, 1), F32)
        out_hi[...] = jnp.zeros(out_hi.shape, BF16)
        out_lo[...] = jnp.zeros(out_lo.shape, BF16)

    def expert_tile(k, r0):
        rows = xs[pl.ds(r0, MOE_TR), :]
        g = jnp.dot(rows, wg_ref[k], preferred_element_type=F32)
        u = jnp.dot(rows, wu_ref[k], preferred_element_type=F32)
        he = (g * jax.nn.sigmoid(g)) * u
        out = jnp.dot(he.astype(BF16), wd_ref[k], preferred_element_type=F32) * ws[pl.ds(r0, MOE_TR), :]
        hi = out.astype(BF16)
        return hi, (out - hi.astype(F32)).astype(BF16)

    def store_tile(r0, hi, lo):
        out_hi[pl.ds(r0, MOE_TR), :] = hi
        out_lo[pl.ds(r0, MOE_TR), :] = lo

    experts = [s * MOE_EXPERTS_PER_STEP + k for k in range(MOE_EXPERTS_PER_STEP)]
    offs = [pl.multiple_of(meta[e], MOE_ALIGN) for e in experts]
    cnts = [meta[N_EXPERTS + e] for e in experts]
    one_tile_each = cnts[0] <= MOE_TR
    for c in cnts[1:]:
        one_tile_each = jnp.logical_and(one_tile_each, c <= MOE_TR)

    @pl.when(one_tile_each)
    def _():
        tiles = [expert_tile(k, offs[k]) for k in range(MOE_EXPERTS_PER_STEP)]
        for k in range(MOE_EXPERTS_PER_STEP):
            store_tile(offs[k], *tiles[k])

    @pl.when(jnp.logical_not(one_tile_each))
    def _():
        for k in range(MOE_EXPERTS_PER_STEP):
            n_tiles = (cnts[k] + (MOE_TR - 1)) // MOE_TR

            @pl.loop(0, n_tiles)
            def _(i):
                r0 = pl.multiple_of(offs[k] + i * MOE_TR, MOE_ALIGN)
                store_tile(r0, *expert_tile(k, r0))

    @pl.when(s == pl.num_programs(1) - 1)
    def _():
        d1c = d1c_s[...]
        d2c = d2c_s[...]
        acc = jnp.zeros((tb, D_MODEL), F32)
        for c in range(pr // MOE_CH):
            pcol = _iota_f32((tb, MOE_CH), 1) + float(c * MOE_CH)
            pt = _one_or_zero(pcol == d1c, pcol == d2c).astype(BF16)
            acc = (acc + jnp.dot(pt, out_hi[c * MOE_CH:(c + 1) * MOE_CH, :], preferred_element_type=F32)
                   + jnp.dot(pt, out_lo[c * MOE_CH:(c + 1) * MOE_CH, :], preferred_element_type=F32))
        o_ref[...] = x1_ref[...] + gt2_ref[...] * acc


def _moe_sorted(h2, route, route_t, x1, gt2, wg_bf, wu_bf, wd_bf):
    rows = h2.shape[0]
    tb = MOE_TB
    pr = -(-(2 * tb + N_EXPERTS * (MOE_ALIGN - 1)) // MOE_CH) * MOE_CH
    eps = MOE_EXPERTS_PER_STEP
    return pl.pallas_call(
        _moe_sorted_kernel,
        grid=(rows // tb, N_EXPERTS // eps),
        in_specs=[pl.BlockSpec((tb, D_MODEL), lambda b, s: (b, 0)),
                  pl.BlockSpec((tb, ROUTER_LANES), lambda b, s: (b, 0)),
                  pl.BlockSpec((SUBLANES, tb), lambda b, s: (0, b)),
                  pl.BlockSpec((tb, D_MODEL), lambda b, s: (b, 0)),
                  pl.BlockSpec((1, D_MODEL), lambda b, s: (0, 0)),
                  pl.BlockSpec((eps, D_MODEL, D_EXPERT), lambda b, s: (s, 0, 0)),
                  pl.BlockSpec((eps, D_MODEL, D_EXPERT), lambda b, s: (s, 0, 0)),
                  pl.BlockSpec((eps, D_EXPERT, D_MODEL), lambda b, s: (s, 0, 0))],
        out_specs=pl.BlockSpec((tb, D_MODEL), lambda b, s: (b, 0)),
        out_shape=jax.ShapeDtypeStruct((rows, D_MODEL), F32),
        scratch_shapes=[pltpu.VMEM((pr + MOE_TR, D_MODEL), BF16), pltpu.VMEM((pr + MOE_TR, D_MODEL), BF16),
                        pltpu.VMEM((pr + MOE_TR, D_MODEL), BF16), pltpu.VMEM((pr + MOE_TR, 1), F32),
                        pltpu.VMEM((tb, 1), F32), pltpu.VMEM((tb, 1), F32),
                        pltpu.SMEM((2 * N_EXPERTS,), jnp.int32)],
        compiler_params=pltpu.CompilerParams(dimension_semantics=("parallel", "arbitrary"),
                                             vmem_limit_bytes=MOE_VMEM_LIMIT),
        name="moe_sorted",
    )(h2, route, route_t, x1, gt2, wg_bf, wu_bf, wd_bf)


def _rope_tables(pos):
    inv = 1.0 / (ROPE_THETA ** (jnp.arange(0, HEAD_DIM, 2, dtype=F32) / HEAD_DIM))
    ang = pos.astype(F32)[:, None] * inv[None, :]
    cos = jnp.cos(ang)
    sin = jnp.sin(ang)
    reps = LANES // (HEAD_DIM // 2)
    cos_t = jnp.tile(cos, (1, reps))
    sin_t = jnp.tile(jnp.concatenate([-sin, sin], axis=-1), (1, reps // 2))
    return cos_t, sin_t


def _lambda_init(layer):
    return 0.8 - 0.6 * math.exp(-0.3 * layer)


def kernel(x_prompt, x_sample, cache_k, cache_v, state_conv, page_table, c_prompt, c_sample, w_ada, b_ada, g_norm1, g_norm2, w_in, g_qnorm, g_knorm, lam_q1, lam_k1, lam_q2, lam_k2, g_subln, w_dw, b_dw, g_ln_conv, b_ln_conv, w_out, w_router_group, b_router_group, w_router_expert, b_router_expert, w_gate_e, w_up_e, w_down_e):
    depth = w_ada.shape[0]
    assert depth == 1, "single-layer trunk"
    layer = 0
    batch_p, seq, _ = x_prompt.shape
    batch_s, n_new, _ = x_sample.shape
    assert batch_p == 1
    past = page_table.shape[1] * PAGE_SIZE
    n_pool = cache_k.shape[1]
    lam_init = _lambda_init(layer)
    row = lambda a: a[layer].reshape(1, -1)

    n_c = batch_p + batch_s
    n_c_pad = -(-n_c // SUBLANES) * SUBLANES
    c_all = jnp.concatenate([c_prompt, c_sample, jnp.zeros((n_c_pad - n_c, D_MODEL), F32)], axis=0)
    mod = _ada(c_all, w_ada[layer], b_ada[layer])
    mod_p = [mod[0:1, i * D_MODEL:(i + 1) * D_MODEL] for i in range(N_MOD)]
    mod_s = [jnp.repeat(mod[1:1 + batch_s, i * D_MODEL:(i + 1) * D_MODEL], n_new, axis=0) for i in range(N_MOD)]

    w_in_bf = w_in[layer].astype(BF16)
    w_out_bf = w_out[layer].astype(BF16)
    wg_bf = w_gate_e[layer].astype(BF16)
    wu_bf = w_up_e[layer].astype(BF16)
    wd_bf = w_down_e[layer].astype(BF16)
    gq = jnp.tile(g_qnorm[layer], QK_WIDTH // HEAD_DIM).reshape(1, -1)
    gk = jnp.tile(g_knorm[layer], QK_WIDTH // HEAD_DIM).reshape(1, -1)
    seg_id = jnp.arange(QK_WIDTH) // HEAD_DIM
    seg = (seg_id[:, None] == seg_id[None, :]).astype(BF16)
    w_r = jnp.concatenate([w_router_expert[layer], w_router_group[layer],
                           jnp.zeros((D_MODEL, ROUTER_LANES - N_EXPERTS - N_GROUPS), F32)], axis=1)
    wr_hi = w_r.astype(BF16)
    wr_lo = (w_r - wr_hi.astype(F32)).astype(BF16)
    b_r = jnp.concatenate([b_router_expert[layer], b_router_group[layer],
                           jnp.zeros((ROUTER_LANES - N_EXPERTS - N_GROUPS,), F32)]).reshape(1, -1)
    lam_p = [row(lam_q1), row(lam_k1), row(lam_q2), row(lam_k2)]
    gs = row(g_subln)
    conv_p = (w_dw[layer], row(b_dw), row(g_ln_conv), row(b_ln_conv))

    xp = x_prompt.reshape(seq, D_MODEL)
    cos_p, sin_p = _rope_tables(jnp.arange(seq))
    k_p, v_p, glu_p, q_bf, k_bf, v_bf = _in_proj(xp, mod_p[0], mod_p[1], row(g_norm1), w_in_bf, gq, gk,
                                                 cos_p, sin_p, seg, tm=512)
    oc_p = _conv_prompt(glu_p, *conv_p, tm=512)
    oa_p = _attn_prompt(q_bf, k_bf, v_bf, lam_p, gs, lam_init, tq=1024, tk=1024)
    x1_p, h2_p, _, route_p = _out_router(xp, oa_p, oc_p, w_out_bf, mod_p[2], mod_p[3], mod_p[4], row(g_norm2),
                                         wr_hi, wr_lo, b_r, tm=512)
    y_p = _moe_sorted(h2_p, route_p, jnp.transpose(route_p[:, :SUBLANES]), x1_p, mod_p[5], wg_bf, wu_bf, wd_bf)

    xs = x_sample.reshape(batch_s * n_new, D_MODEL)
    pos_s = past + (jnp.arange(batch_s * n_new) % n_new)
    cos_s, sin_s = _rope_tables(pos_s)
    rows_s = batch_s * n_new
    k_s, v_s, glu_s, qs_bf, ks_bf, vs_bf = _in_proj(xs, mod_s[0], mod_s[1], row(g_norm1), w_in_bf, gq, gk,
                                                    cos_s, sin_s, seg, tm=rows_s)
    xp_s = jnp.concatenate([state_conv[layer], glu_s.reshape(batch_s, n_new, CONV_CH)], axis=1)
    oc_s = _conv_sample(jnp.transpose(xp_s, (1, 0, 2)), *conv_p, n_new=n_new)
    oc_s = jnp.transpose(oc_s, (1, 0, 2)).reshape(rows_s, CONV_CH)
    tok3 = lambda a: a.reshape(batch_s, n_new, D_ATTN)
    pad_q = lambda a: jnp.pad(tok3(a), ((0, 0), (0, SUBLANES - n_new), (0, 0)))
    oa_s = _attn_sample(page_table, _block_diag_queries(tok3(qs_bf), SUBLANES), pad_q(k_s), pad_q(v_s),
                        jnp.transpose(cache_k[layer], (0, 2, 3, 4, 1)).reshape(n_pool, D_ATTN, PAGE_SIZE),
                        cache_v[layer].reshape(n_pool, PAGE_SIZE * N_HEADS, V_DIM), lam_p, gs, lam_init, n_new)
    oa_s = oa_s[:, :n_new].reshape(rows_s, D_ATTN)
    x1_s, h2_s, gates_s, _ = _out_router(xs, oa_s, oc_s, w_out_bf, mod_s[2], mod_s[3],
                                      mod_s[4], row(g_norm2), wr_hi, wr_lo, b_r, tm=rows_s)
    y_s = _moe(h2_s, gates_s, x1_s, mod_s[5], wg_bf, wu_bf, wd_bf, tm=rows_s)

    hshape = (N_HEADS, 2, HEAD_DIM)
    return (y_p.reshape(batch_p, seq, D_MODEL),
            y_s.reshape(batch_s, n_new, D_MODEL),
            k_p.reshape(depth, batch_p, seq, *hshape),
            v_p.reshape(depth, batch_p, seq, N_HEADS, V_DIM),
            glu_p[seq - (CONV_WIDTH - 1):].reshape(depth, batch_p, CONV_WIDTH - 1, CONV_CH),
            k_s.reshape(depth, batch_s, n_new, *hshape),
            v_s.reshape(depth, batch_s, n_new, N_HEADS, V_DIM),
            xp_s[:, n_new:].reshape(depth, batch_s, CONV_WIDTH - 1, CONV_CH))
```

```python
import functools
import math

import jax
import jax.numpy as jnp
from jax import lax
from jax.experimental import pallas as pl
from jax.experimental.pallas import tpu as pltpu

F32 = jnp.float32
BF16 = jnp.bfloat16

D_MODEL = 1024
HEAD_DIM = 64
V_DIM = 2 * HEAD_DIM
N_HEADS = (D_MODEL // 2) // V_DIM
D_ATTN = N_HEADS * V_DIM
QK_WIDTH = N_HEADS * 2 * HEAD_DIM
CONV_CH = D_MODEL - D_ATTN
CONV_WIDTH = 31
IN_COLS = 2 * QK_WIDTH + D_ATTN + 2 * CONV_CH
N_GROUPS = 4
EXPERTS_PER_GROUP = 8
N_EXPERTS = N_GROUPS * EXPERTS_PER_GROUP
D_EXPERT = D_MODEL // 4
ROPE_THETA = 10000.0
PAGE_SIZE = 128
EPS = 1e-6
N_MOD = 6
NEG_INF = -1e30
QK_SCALE_LOG2 = HEAD_DIM ** -0.5 * math.log2(math.e)

LANES = 128
SUBLANES = 8
VMEM_LIMIT = 48 * 1024 * 1024
HALO = 32
CONV_CHUNK = 64
PAGES_PER_STEP = 32
ROUTER_LANES = 128
MOE_TB = 1024
MOE_TR = 256
MOE_CH = 512
MOE_ALIGN = 16
MOE_EXPERTS_PER_STEP = 2
MOE_VMEM_LIMIT = 56 * 1024 * 1024


def _params(*sem):
    return pltpu.CompilerParams(dimension_semantics=sem, vmem_limit_bytes=VMEM_LIMIT)


def _nt_dot(a, b):
    return lax.dot_general(a, b, (((1,), (1,)), ((), ())), preferred_element_type=F32)


def _ada_kernel(c_ref, w_ref, b_ref, o_ref):
    c = c_ref[...]
    a = (c * jax.nn.sigmoid(c)).astype(BF16)
    o_ref[...] = jnp.dot(a, w_ref[...].astype(BF16), preferred_element_type=F32) + b_ref[...]


def _ada(c_all, w_ada, b_ada):
    rows = c_all.shape[0]
    return pl.pallas_call(
        _ada_kernel,
        grid=(N_MOD,),
        in_specs=[
            pl.BlockSpec((rows, D_MODEL), lambda j: (0, 0)),
            pl.BlockSpec((D_MODEL, D_MODEL), lambda j: (0, j)),
            pl.BlockSpec((1, D_MODEL), lambda j: (0, j)),
        ],
        out_specs=pl.BlockSpec((rows, D_MODEL), lambda j: (0, j)),
        out_shape=jax.ShapeDtypeStruct((rows, N_MOD * D_MODEL), F32),
        compiler_params=_params("arbitrary"),
        name="ada",
    )(c_all, w_ada, b_ada.reshape(1, -1))


def _rms_rows(x, g):
    return x * lax.rsqrt(jnp.mean(x * x, axis=-1, keepdims=True) + EPS) * g


def _qk_norm_rope(t, g, seg, cos, sin_signed, first_half):
    ms = jnp.dot((t * t).astype(BF16), seg, preferred_element_type=F32) * (1.0 / HEAD_DIM)
    tn = t * lax.rsqrt(ms + EPS) * g
    out = []
    for i in range(t.shape[1] // LANES):
        xs = tn[:, i * LANES:(i + 1) * LANES]
        rot = jnp.where(first_half, pltpu.roll(xs, LANES - HEAD_DIM // 2, 1), pltpu.roll(xs, HEAD_DIM // 2, 1))
        out.append(xs * cos + rot * sin_signed)
    return jnp.concatenate(out, axis=1)


def _in_proj_kernel(x_ref, sh_ref, sc_ref, g1_ref, w_ref, gq_ref, gk_ref, cos_ref, sin_ref, seg_ref,
                    k_out, v_out, glu_out, q_bf, k_bf, v_bf):
    x = x_ref[...]
    h = _rms_rows(x, g1_ref[...]) * (1.0 + sc_ref[...]) + sh_ref[...]
    proj = jnp.dot(h.astype(BF16), w_ref[...], preferred_element_type=F32)
    cos = cos_ref[...]
    sin = sin_ref[...]
    seg = seg_ref[...]
    lane = lax.broadcasted_iota(jnp.int32, cos.shape, 1)
    first_half = (lane % HEAD_DIM) < (HEAD_DIM // 2)
    q = _qk_norm_rope(proj[:, :QK_WIDTH], gq_ref[...], seg, cos, sin, first_half)
    k = _qk_norm_rope(proj[:, QK_WIDTH:2 * QK_WIDTH], gk_ref[...], seg, cos, sin, first_half)
    v = proj[:, 2 * QK_WIDTH:2 * QK_WIDTH + D_ATTN]
    a = proj[:, 2 * QK_WIDTH + D_ATTN:2 * QK_WIDTH + D_ATTN + CONV_CH]
    gate = proj[:, 2 * QK_WIDTH + D_ATTN + CONV_CH:]
    k_out[...] = k
    for hd in range(N_HEADS):
        v_out[pl.ds(hd, v.shape[0], stride=N_HEADS), :] = v[:, hd * V_DIM:(hd + 1) * V_DIM]
    glu_out[...] = a * jax.nn.sigmoid(gate)
    q_bf[...] = (q * QK_SCALE_LOG2).astype(BF16)
    k_bf[...] = k.astype(BF16)
    v_bf[...] = v.astype(BF16)


def _in_proj(x, sh, sc, g1, w_in_bf, gq, gk, cos, sin, seg, tm):
    rows = x.shape[0]
    per_row = sh.shape[0] != 1
    mod_spec = pl.BlockSpec((tm, D_MODEL), lambda i: (i, 0)) if per_row else pl.BlockSpec((1, D_MODEL), lambda i: (0, 0))
    const = lambda shape: pl.BlockSpec(shape, lambda i: (0, 0))
    row_spec = lambda width: pl.BlockSpec((tm, width), lambda i: (i, 0))
    f32_out = jax.ShapeDtypeStruct((rows, QK_WIDTH), F32)
    bf_out = jax.ShapeDtypeStruct((rows, QK_WIDTH), BF16)
    return pl.pallas_call(
        _in_proj_kernel,
        grid=(rows // tm,),
        in_specs=[row_spec(D_MODEL), mod_spec, mod_spec, const((1, D_MODEL)), const((D_MODEL, IN_COLS)),
                  const((1, QK_WIDTH)), const((1, QK_WIDTH)), row_spec(LANES), row_spec(LANES),
                  const((QK_WIDTH, QK_WIDTH))],
        out_specs=[row_spec(QK_WIDTH), pl.BlockSpec((tm * N_HEADS, V_DIM), lambda i: (i, 0))] + [row_spec(QK_WIDTH)] * 4,
        out_shape=[f32_out, jax.ShapeDtypeStruct((rows * N_HEADS, V_DIM), F32), f32_out, bf_out, bf_out, bf_out],
        compiler_params=_params("parallel"),
        name="in_proj",
    )(x, sh, sc, g1, w_in_bf, gq, gk, cos, sin, seg)


def _ln_swish(y, g, b):
    mu = jnp.mean(y, axis=-1, keepdims=True)
    d = y - mu
    var = jnp.mean(d * d, axis=-1, keepdims=True)
    z = d * lax.rsqrt(var + EPS) * g + b
    return z * jax.nn.sigmoid(z)


def _conv_prompt_kernel(glu_ref, w_ref, bdw_ref, g_ref, b_ref, o_ref, xpad, shifted):
    tm = glu_ref.shape[0]

    @pl.when(pl.program_id(0) == 0)
    def _():
        xpad[0:HALO, :] = jnp.zeros((HALO, CONV_CH), F32)

    xpad[HALO:HALO + tm, :] = glu_ref[...]
    base = HALO - (CONV_WIDTH - 1)
    for c in range(tm // CONV_CHUNK):
        r0 = c * CONV_CHUNK
        acc = jnp.zeros((CONV_CHUNK, CONV_CH), F32)
        for phase in range(SUBLANES):
            taps = [j for j in range(CONV_WIDTH) if (base + j) % SUBLANES == phase]
            reach = max(base + j - phase for j in taps) + CONV_CHUNK
            shifted[phase, 0:reach, :] = xpad[r0 + phase:r0 + phase + reach, :]
            for j in taps:
                a = base + j - phase
                acc = acc + shifted[phase, a:a + CONV_CHUNK, :] * w_ref[j:j + 1, :]
        y = acc + bdw_ref[...]
        o_ref[r0:r0 + CONV_CHUNK, :] = _ln_swish(y, g_ref[...], b_ref[...]).astype(BF16)
    xpad[0:HALO, :] = xpad[tm:tm + HALO, :]


def _conv_prompt(glu, w_dw, b_dw, g_ln, b_ln, tm):
    rows = glu.shape[0]
    const = lambda shape: pl.BlockSpec(shape, lambda i: (0, 0))
    return pl.pallas_call(
        _conv_prompt_kernel,
        grid=(rows // tm,),
        in_specs=[pl.BlockSpec((tm, CONV_CH), lambda i: (i, 0)), const((CONV_WIDTH, CONV_CH)),
                  const((1, CONV_CH)), const((1, CONV_CH)), const((1, CONV_CH))],
        out_specs=pl.BlockSpec((tm, CONV_CH), lambda i: (i, 0)),
        out_shape=jax.ShapeDtypeStruct((rows, CONV_CH), BF16),
        scratch_shapes=[pltpu.VMEM((tm + HALO, CONV_CH), F32),
                        pltpu.VMEM((SUBLANES, CONV_CHUNK + HALO, CONV_CH), F32)],
        compiler_params=_params("arbitrary"),
        name="conv_prompt",
    )(glu, w_dw, b_dw, g_ln, b_ln)


def _conv_sample_kernel(xp_ref, w_ref, bdw_ref, g_ref, b_ref, o_ref):
    n_new = o_ref.shape[0]
    for i in range(n_new):
        acc = jnp.zeros(o_ref.shape[1:], F32)
        for j in range(CONV_WIDTH):
            acc = acc + xp_ref[i + j] * w_ref[j:j + 1, :]
        o_ref[i] = _ln_swish(acc + bdw_ref[...], g_ref[...], b_ref[...]).astype(BF16)


def _conv_sample(xp_t, w_dw, b_dw, g_ln, b_ln, n_new):
    t, batch, ch = xp_t.shape
    const2 = lambda shape: pl.BlockSpec(shape, lambda i: (0, 0))
    return pl.pallas_call(
        _conv_sample_kernel,
        grid=(1,),
        in_specs=[pl.BlockSpec((t, batch, ch), lambda i: (0, 0, 0)), const2((CONV_WIDTH, ch)),
                  const2((1, ch)), const2((1, ch)), const2((1, ch))],
        out_specs=pl.BlockSpec((n_new, batch, ch), lambda i: (0, 0, 0)),
        out_shape=jax.ShapeDtypeStruct((n_new, batch, ch), BF16),
        compiler_params=_params("arbitrary"),
        name="conv_sample",
    )(xp_t, w_dw, b_dw, g_ln, b_ln)


def _lambda(lq1, lk1, lq2, lk2, lam_init):
    s1 = jnp.sum(lq1[...] * lk1[...], axis=-1, keepdims=True)
    s2 = jnp.sum(lq2[...] * lk2[...], axis=-1, keepdims=True)
    return jnp.exp(s1) - jnp.exp(s2) + lam_init


def _online_step(s, vb, m, l, acc):
    m_new = jnp.maximum(m, jnp.max(s, axis=-1, keepdims=True))
    alpha = jnp.exp2(m - m_new)
    e = jnp.exp2(s - m_new)
    l = alpha * l + jnp.sum(e, axis=-1, keepdims=True)
    acc = alpha * acc + jnp.dot(e.astype(BF16), vb, preferred_element_type=F32)
    return m_new, l, acc


def _attn_prompt_kernel(q_ref, k_ref, v_ref, lq1, lk1, lq2, lk2, gs_ref, o_ref, m_scr, l_scr, acc_scr, lp_scr,
                        kmax_scr, *, lam_init, tk):
    tq = q_ref.shape[0]
    qi = pl.program_id(1)
    q = q_ref[...]
    lane = lax.broadcasted_iota(jnp.int32, q.shape, 1)
    zero = jnp.zeros_like(q)
    q2 = jnp.concatenate([jnp.where(lane < HEAD_DIM, q, zero), jnp.where(lane >= HEAD_DIM, q, zero)], axis=0)

    @pl.when(qi == 0)
    def _():
        def chunk_max(c, carry):
            kb = k_ref[pl.ds(pl.multiple_of(c * tk, tk), tk), :].astype(F32)
            kk = kb * kb
            lane_k = lax.broadcasted_iota(jnp.int32, kk.shape, 1)
            n0 = jnp.sum(jnp.where(lane_k < HEAD_DIM, kk, 0.0), axis=-1, keepdims=True)
            n1 = jnp.sum(jnp.where(lane_k >= HEAD_DIM, kk, 0.0), axis=-1, keepdims=True)
            return (jnp.maximum(carry[0], jnp.max(n0, axis=0, keepdims=True)),
                    jnp.maximum(carry[1], jnp.max(n1, axis=0, keepdims=True)))

        k0, k1 = lax.fori_loop(0, k_ref.shape[0] // tk, chunk_max, (jnp.zeros((1, 1), F32), jnp.zeros((1, 1), F32)))
        kmax_scr[0:1, :] = jnp.broadcast_to(k0, (1, LANES))
        kmax_scr[1:2, :] = jnp.broadcast_to(k1, (1, LANES))

    qf = q2.astype(F32)
    kmax = jnp.concatenate([jnp.broadcast_to(kmax_scr[0:1, 0:1], (tq, 1)),
                            jnp.broadcast_to(kmax_scr[1:2, 0:1], (tq, 1))], axis=0)
    bound = jnp.sqrt(jnp.sum(qf * qf, axis=-1, keepdims=True) * kmax)

    def scores(j, masked):
        off = pl.multiple_of(j * tk, tk)
        s = _nt_dot(q2, k_ref[pl.ds(off, tk), :])
        if masked:
            row = lax.broadcasted_iota(jnp.int32, s.shape, 0) % tq + qi * tq
            col = lax.broadcasted_iota(jnp.int32, s.shape, 1) + off
            s = jnp.where(col <= row, s, NEG_INF)
        return s, v_ref[pl.ds(off, tk), :]

    def fold_lanes(e):
        part = e[:, 0:LANES]
        for t in range(1, e.shape[1] // LANES):
            part = part + e[:, t * LANES:(t + 1) * LANES]
        return part

    def fast(j, masked):
        if masked and tq == tk:
            return fast_diagonal(j)
        s, vb = scores(j, masked)
        e = jnp.exp2(s - bound)
        lp_scr[...] += fold_lanes(e)
        acc_scr[...] += jnp.dot(e.astype(BF16), vb, preferred_element_type=F32)

    def fast_diagonal(j):
        hq = tq // 2
        off = pl.multiple_of(j * tk, tk)
        s = _nt_dot(q2, k_ref[pl.ds(off, hq), :])
        row = lax.broadcasted_iota(jnp.int32, s.shape, 0) % tq
        col = lax.broadcasted_iota(jnp.int32, s.shape, 1)
        e = jnp.exp2(jnp.where(col <= row, s, NEG_INF) - bound)
        lp_scr[...] += fold_lanes(e)
        acc_scr[...] += jnp.dot(e.astype(BF16), v_ref[pl.ds(off, hq), :], preferred_element_type=F32)

        late = [slice(hq, tq), slice(tq + hq, 2 * tq)]
        q_late = jnp.concatenate([q2[r] for r in late], axis=0)
        b_late = jnp.concatenate([bound[r] for r in late], axis=0)
        s = _nt_dot(q_late, k_ref[pl.ds(pl.multiple_of(off + hq, hq), hq), :])
        row = lax.broadcasted_iota(jnp.int32, s.shape, 0) % hq
        col = lax.broadcasted_iota(jnp.int32, s.shape, 1)
        e = jnp.exp2(jnp.where(col <= row, s, NEG_INF) - b_late)
        part = fold_lanes(e)
        pv = jnp.dot(e.astype(BF16), v_ref[pl.ds(pl.multiple_of(off + hq, hq), hq), :], preferred_element_type=F32)
        for n, r in enumerate(late):
            lp_scr[r, :] += part[n * hq:(n + 1) * hq]
            acc_scr[r, :] += pv[n * hq:(n + 1) * hq]

    def exact(j, masked):
        s, vb = scores(j, masked)
        m, l, acc = _online_step(s, vb, m_scr[...], l_scr[...], acc_scr[...])
        m_scr[...] = m
        l_scr[...] = l
        acc_scr[...] = acc

    def finish(l):
        lam = _lambda(lq1, lk1, lq2, lk2, lam_init)
        acc = acc_scr[...]
        o = acc[:tq] / l[:tq] - lam * (acc[tq:] / l[tq:])
        o_ref[...] = (_rms_rows(o, gs_ref[...]) * (1.0 - lam_init)).astype(BF16)

    n_full = (qi * tq) // tk
    n_pairs = n_full // 2
    last = 2 * n_pairs
    lp_scr[...] = jnp.zeros(lp_scr.shape, F32)
    acc_scr[...] = jnp.zeros(acc_scr.shape, F32)

    @pl.loop(0, n_pairs)
    def _(p):
        fast(2 * p, False)
        fast(2 * p + 1, False)

    @pl.when(n_full == last)
    def _():
        fast(last, True)

    @pl.when(n_full != last)
    def _():
        fast(last, False)
        fast(last + 1, True)

    l_fast = jnp.sum(lp_scr[...], axis=-1, keepdims=True)
    trusted = jnp.logical_and(jnp.min(l_fast) > 2.0 ** -60, jnp.max(l_fast) < 2.0 ** 100)

    @pl.when(trusted)
    def _():
        finish(l_fast)

    @pl.when(jnp.logical_not(trusted))
    def _():
        m_scr[...] = jnp.full(m_scr.shape, NEG_INF, F32)
        l_scr[...] = jnp.zeros(l_scr.shape, F32)
        acc_scr[...] = jnp.zeros(acc_scr.shape, F32)

        @pl.loop(0, n_full)
        def _(j):
            exact(j, False)

        exact(n_full, True)
        finish(l_scr[...])


def _attn_prompt(q_bf, k_bf, v_bf, lam_p, g_subln, lam_init, tq, tk):
    s = q_bf.shape[0]
    assert tk % tq == 0 and s % tk == 0
    vec = lambda n: pl.BlockSpec((1, n), lambda h, i: (0, 0))
    return pl.pallas_call(
        functools.partial(_attn_prompt_kernel, lam_init=lam_init, tk=tk),
        grid=(N_HEADS, s // tq),
        in_specs=[pl.BlockSpec((tq, V_DIM), lambda h, i: (i, h)),
                  pl.BlockSpec((s, V_DIM), lambda h, i: (0, h)),
                  pl.BlockSpec((s, V_DIM), lambda h, i: (0, h)),
                  vec(HEAD_DIM), vec(HEAD_DIM), vec(HEAD_DIM), vec(HEAD_DIM), vec(V_DIM)],
        out_specs=pl.BlockSpec((tq, V_DIM), lambda h, i: (i, h)),
        out_shape=jax.ShapeDtypeStruct((s, D_ATTN), BF16),
        scratch_shapes=[pltpu.VMEM((2 * tq, 1), F32), pltpu.VMEM((2 * tq, 1), F32),
                        pltpu.VMEM((2 * tq, V_DIM), F32), pltpu.VMEM((2 * tq, LANES), F32),
                        pltpu.VMEM((SUBLANES, LANES), F32)],
        compiler_params=_params("arbitrary", "arbitrary"),
        name="attn_prompt",
    )(q_bf, k_bf, v_bf, *lam_p, g_subln)


def _attn_sample_kernel(pt_ref, qbd_ref, kn_ref, vn_ref, *rest, lam_init, n_new):
    del pt_ref
    npg = PAGES_PER_STEP
    kp = rest[:npg]
    vp = rest[npg:2 * npg]
    lq1, lk1, lq2, lk2, gs_ref, o_ref, m_s, l_s, acc_s = rest[2 * npg:]
    j = pl.program_id(1)
    n_rows = qbd_ref.shape[1]
    hrows = n_rows // N_HEADS
    qpad = hrows // 2

    @pl.when(j == 0)
    def _():
        m_s[...] = jnp.full(m_s.shape, NEG_INF, F32)
        l_s[...] = jnp.zeros(l_s.shape, F32)
        acc_s[...] = jnp.zeros(acc_s.shape, F32)

    qbd = qbd_ref[0]
    kc = jnp.concatenate([r[0] for r in kp], axis=1).astype(BF16)
    s = jnp.dot(qbd, kc, preferred_element_type=F32)
    m = m_s[...]
    m_new = jnp.maximum(m, jnp.max(s, axis=-1, keepdims=True))
    alpha = jnp.exp2(m - m_new)
    e = jnp.exp2(s - m_new)
    l = alpha * l_s[...] + jnp.sum(e, axis=-1, keepdims=True)
    e = e.astype(BF16)
    pv = []
    for h in range(N_HEADS):
        vh = jnp.concatenate([r[0, pl.ds(h, PAGE_SIZE, stride=N_HEADS), :] for r in vp], axis=0).astype(BF16)
        pv.append(jnp.dot(e[h * hrows:(h + 1) * hrows], vh, preferred_element_type=F32))
    acc = alpha * acc_s[...] + jnp.concatenate(pv, axis=0)
    m_s[...] = m_new
    l_s[...] = l
    acc_s[...] = acc

    @pl.when(j == pl.num_programs(1) - 1)
    def _():
        qf = qbd.astype(F32)
        kn = kn_ref[0].astype(BF16).astype(F32)
        vn = vn_ref[0].astype(BF16).astype(F32)
        qpos = lax.broadcasted_iota(jnp.int32, (n_rows, 1), 0) % qpad
        s_new = [jnp.where(t <= qpos, jnp.sum(qf * kn[t:t + 1, :], axis=-1, keepdims=True), NEG_INF)
                 for t in range(n_new)]
        m_f = m_new
        for s_t in s_new:
            m_f = jnp.maximum(m_f, s_t)
        a_f = jnp.exp2(m_new - m_f)
        l_f = a_f * l
        acc_f = a_f * acc
        e_new = [jnp.exp2(s_t - m_f) for s_t in s_new]
        for e_t in e_new:
            l_f = l_f + e_t
        lam = _lambda(lq1, lk1, lq2, lk2, lam_init)
        gs = gs_ref[...]
        outs = []
        for h in range(N_HEADS):
            a_h = acc_f[h * hrows:(h + 1) * hrows]
            for t in range(n_new):
                e_t = e_new[t][h * hrows:(h + 1) * hrows].astype(BF16).astype(F32)
                a_h = a_h + e_t * vn[t:t + 1, h * V_DIM:(h + 1) * V_DIM]
            o_h = a_h / l_f[h * hrows:(h + 1) * hrows]
            outs.append(_rms_rows(o_h[:qpad] - lam * o_h[qpad:], gs))
        o_ref[0] = (jnp.concatenate(outs, axis=1) * (1.0 - lam_init)).astype(BF16)


def _attn_sample(page_table, qbd, kn_pad, vn_pad, cache_kt, cache_v, lam_p, g_subln, lam_init, n_new):
    batch, n_rows, _ = qbd.shape
    qpad = kn_pad.shape[1]
    n_pages = page_table.shape[1]
    steps = n_pages // PAGES_PER_STEP
    pt_flat = page_table.reshape(-1)
    tok = lambda r: pl.BlockSpec((1, r, D_ATTN), lambda b, j, pt: (b, 0, 0))

    def page_spec(p, shape):
        return pl.BlockSpec((1,) + shape, lambda b, j, pt: (pt[b * n_pages + j * PAGES_PER_STEP + p], 0, 0))

    vec = lambda n: pl.BlockSpec((1, n), lambda b, j, pt: (0, 0))
    k_pages = [page_spec(p, (D_ATTN, PAGE_SIZE)) for p in range(PAGES_PER_STEP)]
    v_pages = [page_spec(p, (PAGE_SIZE * N_HEADS, V_DIM)) for p in range(PAGES_PER_STEP)]
    grid_spec = pltpu.PrefetchScalarGridSpec(
        num_scalar_prefetch=1,
        grid=(batch, steps),
        in_specs=[tok(n_rows), tok(qpad), tok(qpad)] + k_pages + v_pages
                 + [vec(HEAD_DIM), vec(HEAD_DIM), vec(HEAD_DIM), vec(HEAD_DIM), vec(V_DIM)],
        out_specs=tok(qpad),
        scratch_shapes=[pltpu.VMEM((n_rows, 1), F32), pltpu.VMEM((n_rows, 1), F32),
                        pltpu.VMEM((n_rows, V_DIM), F32)],
    )
    return pl.pallas_call(
        functools.partial(_attn_sample_kernel, lam_init=lam_init, n_new=n_new),
        grid_spec=grid_spec,
        out_shape=jax.ShapeDtypeStruct((batch, qpad, D_ATTN), BF16),
        compiler_params=_params("parallel", "arbitrary"),
        name="attn_sample",
    )(pt_flat, qbd, kn_pad, vn_pad, *([cache_kt] * PAGES_PER_STEP), *([cache_v] * PAGES_PER_STEP),
      *lam_p, g_subln)


def _block_diag_queries(q_bf, qpad):
    batch, n_new, _ = q_bf.shape
    qp = jnp.pad(q_bf, ((0, 0), (0, qpad - n_new), (0, 0)))
    qt = jnp.tile(qp, (1, 2 * N_HEADS, 1))
    chunk = jnp.arange(2 * N_HEADS * qpad) // qpad
    keep = (jnp.arange(D_ATTN)[None, :] // HEAD_DIM) == chunk[:, None]
    return jnp.where(keep[None], qt, jnp.zeros_like(qt))


def _lane_min_where(cond, lane, width):
    return jnp.min(jnp.where(cond, lane, width), axis=-1, keepdims=True)


def _out_router_kernel(x_ref, oa_ref, oc_ref, w_ref, gt1_ref, sh2_ref, sc2_ref, g2_ref,
                       wr_hi_ref, wr_lo_ref, br_ref, x1_out, h2_out, gates_out, route_out):
    mix = (jnp.dot(oa_ref[...], w_ref[0:D_ATTN, :], preferred_element_type=F32)
           + jnp.dot(oc_ref[...], w_ref[D_ATTN:, :], preferred_element_type=F32))
    x1 = x_ref[...] + gt1_ref[...] * mix
    x1_out[...] = x1
    h2 = _rms_rows(x1, g2_ref[...]) * (1.0 + sc2_ref[...]) + sh2_ref[...]
    h2_out[...] = h2.astype(BF16)

    h_hi = h2.astype(BF16)
    h_lo = (h2 - h_hi.astype(F32)).astype(BF16)
    logits = (jnp.dot(h_hi, wr_hi_ref[...], preferred_element_type=F32)
              + jnp.dot(h_hi, wr_lo_ref[...], preferred_element_type=F32)
              + jnp.dot(h_lo, wr_hi_ref[...], preferred_element_type=F32)) + br_ref[...]
    lane_i = lax.broadcasted_iota(jnp.int32, logits.shape, 1)
    lane = lane_i.astype(F32)
    lg = jnp.where(lane_i >= N_EXPERTS, jnp.where(lane_i < N_EXPERTS + N_GROUPS, logits, NEG_INF), NEG_INF)
    mg = jnp.max(lg, axis=-1, keepdims=True)
    pg_top = 1.0 / jnp.sum(jnp.exp(lg - mg), axis=-1, keepdims=True)
    g_idx = _lane_min_where(lg == mg, lane, float(ROUTER_LANES)) - N_EXPERTS
    lane_group = (lane_i // EXPERTS_PER_GROUP).astype(F32)
    le = jnp.where(lane_i < N_EXPERTS, jnp.where(lane_group == g_idx, logits, NEG_INF), NEG_INF)
    m1 = jnp.max(le, axis=-1, keepdims=True)
    i1 = _lane_min_where(le == m1, lane, float(ROUTER_LANES))
    le2 = jnp.where(lane == i1, NEG_INF, le)
    m2 = jnp.max(le2, axis=-1, keepdims=True)
    i2 = _lane_min_where(le2 == m2, lane, float(ROUTER_LANES))
    r = jnp.exp(m2 - m1)
    w1 = 1.0 / (1.0 + r)
    w2 = r / (1.0 + r)
    g1 = pg_top * w1
    g2 = pg_top * w2
    gates_out[...] = jnp.where(lane == i1, g1, 0.0) + jnp.where(lane == i2, g2, 0.0)
    route_out[...] = jnp.where(lane_i == 0, i1, jnp.where(lane_i == 1, i2, jnp.where(lane_i == 2, g1,
                               jnp.where(lane_i == 3, g2, 0.0))))


def _out_router(x, oa, oc, w_out_bf, gt1, sh2, sc2, g2, wr_hi, wr_lo, br, tm):
    rows = x.shape[0]
    per_row = gt1.shape[0] != 1
    mod_spec = pl.BlockSpec((tm, D_MODEL), lambda i: (i, 0)) if per_row else pl.BlockSpec((1, D_MODEL), lambda i: (0, 0))
    const = lambda shape: pl.BlockSpec(shape, lambda i: (0, 0))
    row_spec = lambda width: pl.BlockSpec((tm, width), lambda i: (i, 0))
    return pl.pallas_call(
        _out_router_kernel,
        grid=(rows // tm,),
        in_specs=[row_spec(D_MODEL), row_spec(D_ATTN), row_spec(CONV_CH), const((D_MODEL, D_MODEL)),
                  mod_spec, mod_spec, mod_spec, const((1, D_MODEL)),
                  const((D_MODEL, ROUTER_LANES)), const((D_MODEL, ROUTER_LANES)), const((1, ROUTER_LANES))],
        out_specs=[row_spec(D_MODEL), row_spec(D_MODEL), row_spec(ROUTER_LANES), row_spec(ROUTER_LANES)],
        out_shape=[jax.ShapeDtypeStruct((rows, D_MODEL), F32), jax.ShapeDtypeStruct((rows, D_MODEL), BF16),
                   jax.ShapeDtypeStruct((rows, ROUTER_LANES), F32), jax.ShapeDtypeStruct((rows, ROUTER_LANES), F32)],
        compiler_params=_params("parallel"),
        name="out_router",
    )(x, oa, oc, w_out_bf, gt1, sh2, sc2, g2, wr_hi, wr_lo, br)


def _moe_kernel(h_ref, gates_ref, x1_ref, gt2_ref, wg_ref, wu_ref, wd_ref, y_ref, acc):
    e = pl.program_id(1)

    @pl.when(e == 0)
    def _():
        acc[...] = jnp.zeros(acc.shape, F32)

    t = h_ref[...]
    g = jnp.dot(t, wg_ref[0], preferred_element_type=F32)
    u = jnp.dot(t, wu_ref[0], preferred_element_type=F32)
    he = (g * jax.nn.sigmoid(g)) * u
    out = jnp.dot(he.astype(BF16), wd_ref[0], preferred_element_type=F32)
    gates = gates_ref[...]
    lane = lax.broadcasted_iota(jnp.int32, gates.shape, 1)
    ge = jnp.sum(jnp.where(lane == e, gates, 0.0), axis=-1, keepdims=True)
    acc[...] += ge * out

    @pl.when(e == pl.num_programs(1) - 1)
    def _():
        y_ref[...] = x1_ref[...] + gt2_ref[...] * acc[...]


def _moe(h2, gates, x1, gt2, wg_bf, wu_bf, wd_bf, tm):
    rows = h2.shape[0]
    per_row = gt2.shape[0] != 1
    mod_spec = (pl.BlockSpec((tm, D_MODEL), lambda i, e: (i, 0)) if per_row
                else pl.BlockSpec((1, D_MODEL), lambda i, e: (0, 0)))
    row_spec = lambda width: pl.BlockSpec((tm, width), lambda i, e: (i, 0))
    return pl.pallas_call(
        _moe_kernel,
        grid=(rows // tm, N_EXPERTS),
        in_specs=[row_spec(D_MODEL), row_spec(ROUTER_LANES), row_spec(D_MODEL), mod_spec,
                  pl.BlockSpec((1, D_MODEL, D_EXPERT), lambda i, e: (e, 0, 0)),
                  pl.BlockSpec((1, D_MODEL, D_EXPERT), lambda i, e: (e, 0, 0)),
                  pl.BlockSpec((1, D_EXPERT, D_MODEL), lambda i, e: (e, 0, 0))],
        out_specs=row_spec(D_MODEL),
        out_shape=jax.ShapeDtypeStruct((rows, D_MODEL), F32),
        scratch_shapes=[pltpu.VMEM((tm, D_MODEL), F32)],
        compiler_params=_params("parallel", "arbitrary"),
        name="moe",
    )(h2, gates, x1, gt2, wg_bf, wu_bf, wd_bf)


def _iota_f32(shape, dim):
    return lax.broadcasted_iota(jnp.int32, shape, dim).astype(F32)


def _one_or_zero(a, b):
    return jnp.where(a, 1.0, jnp.where(b, 1.0, 0.0))


def _moe_sorted_kernel(h_ref, route_ref, route_t_ref, x1_ref, gt2_ref, wg_ref, wu_ref, wd_ref, o_ref,
                       xs, out_hi, out_lo, ws, d1c_s, d2c_s, meta):
    s = pl.program_id(1)
    tb = h_ref.shape[0]
    pr = xs.shape[0] - MOE_TR

    @pl.when(s == 0)
    def _():
        rt = route_t_ref[...]
        i1r, i2r, g1r, g2r = rt[0:1], rt[1:2], rt[2:3], rt[3:4]
        eio = _iota_f32((ROUTER_LANES, tb), 0)
        s1t = eio == i1r
        s2t = eio == i2r
        mt = _one_or_zero(s1t, s2t).astype(BF16)
        ranks = []
        for c in range(tb // MOE_TR):
            before = _iota_f32((tb, MOE_TR), 0) < _iota_f32((tb, MOE_TR), 1) + float(c * MOE_TR)
            ranks.append(jnp.dot(mt, jnp.where(before, 1.0, 0.0).astype(BF16), preferred_element_type=F32))
        rank_t = jnp.concatenate(ranks, axis=1)
        cnt_col = jnp.sum(mt.astype(F32), axis=1, keepdims=True)
        seg_col = jnp.floor((cnt_col + (MOE_ALIGN - 1.0)) * (1.0 / MOE_ALIGN))
        lower = jnp.where(_iota_f32((ROUTER_LANES, ROUTER_LANES), 1) < _iota_f32((ROUTER_LANES, ROUTER_LANES), 0),
                          1.0, 0.0).astype(BF16)
        off_col = MOE_ALIGN * jnp.dot(lower, jnp.broadcast_to(seg_col, (ROUTER_LANES, ROUTER_LANES)).astype(BF16),
                                      preferred_element_type=F32)[:, 0:1]
        dest_t = off_col + rank_t
        d1r = jnp.sum(jnp.where(s1t, dest_t, 0.0), axis=0, keepdims=True)
        d2r = jnp.sum(jnp.where(s2t, dest_t, 0.0), axis=0, keepdims=True)

        r = route_ref[...]
        lio = _iota_f32((tb, ROUTER_LANES), 1)
        s1 = lio == r[:, 0:1]
        s2 = lio == r[:, 1:2]
        m = _one_or_zero(s1, s2).astype(BF16)
        ranks = []
        for c in range(tb // MOE_TR):
            before = _iota_f32((MOE_TR, tb), 1) < _iota_f32((MOE_TR, tb), 0) + float(c * MOE_TR)
            ranks.append(jnp.dot(jnp.where(before, 1.0, 0.0).astype(BF16), m, preferred_element_type=F32))
        rank = jnp.concatenate(ranks, axis=0)
        cnt_row = jnp.sum(m.astype(F32), axis=0, keepdims=True)
        seg_row = jnp.floor((cnt_row + (MOE_ALIGN - 1.0)) * (1.0 / MOE_ALIGN))
        upper = jnp.where(_iota_f32((ROUTER_LANES, ROUTER_LANES), 0) < _iota_f32((ROUTER_LANES, ROUTER_LANES), 1),
                          1.0, 0.0).astype(BF16)
        off_row = MOE_ALIGN * jnp.dot(jnp.broadcast_to(seg_row, (SUBLANES, ROUTER_LANES)).astype(BF16), upper,
                                      preferred_element_type=F32)[0:1, :]
        dest = off_row + rank
        d1c_s[...] = jnp.sum(jnp.where(s1, dest, 0.0), axis=-1, keepdims=True)
        d2c_s[...] = jnp.sum(jnp.where(s2, dest, 0.0), axis=-1, keepdims=True)
        off_i = off_row.astype(jnp.int32)
        cnt_i = cnt_row.astype(jnp.int32)
        for e in range(N_EXPERTS):
            meta[e] = off_i[0, e]
            meta[N_EXPERTS + e] = cnt_i[0, e]

        x = h_ref[...]
        for c in range(pr // MOE_CH):
            prow = _iota_f32((MOE_CH, tb), 0) + float(c * MOE_CH)
            h1 = prow == d1r
            h2 = prow == d2r
            xs[c * MOE_CH:(c + 1) * MOE_CH, :] = jnp.dot(_one_or_zero(h1, h2).astype(BF16), x,
                                                       preferred_element_type=F32).astype(BF16)
            ws[c * MOE_CH:(c + 1) * MOE_CH, :] = jnp.sum(jnp.where(h1, g1r, 0.0) + jnp.where(h2, g2r, 0.0),
                                                       axis=-1, keepdims=True)
        xs[pr:pr + MOE_TR, :] = jnp.zeros((MOE_TR, D_MODEL), BF16)
        ws[pr:pr + MOE_TR, :] = jnp.zeros((MOE_TR, 1), F32)
        out_hi[...] = jnp.zeros(out_hi.shape, BF16)
        out_lo[...] = jnp.zeros(out_lo.shape, BF16)

    def expert_tile(k, r0):
        rows = xs[pl.ds(r0, MOE_TR), :]
        g = jnp.dot(rows, wg_ref[k], preferred_element_type=F32)
        u = jnp.dot(rows, wu_ref[k], preferred_element_type=F32)
        he = (g * jax.nn.sigmoid(g)) * u
        out = jnp.dot(he.astype(BF16), wd_ref[k], preferred_element_type=F32) * ws[pl.ds(r0, MOE_TR), :]
        hi = out.astype(BF16)
        return hi, (out - hi.astype(F32)).astype(BF16)

    def store_tile(r0, hi, lo):
        out_hi[pl.ds(r0, MOE_TR), :] = hi
        out_lo[pl.ds(r0, MOE_TR), :] = lo

    experts = [s * MOE_EXPERTS_PER_STEP + k for k in range(MOE_EXPERTS_PER_STEP)]
    offs = [pl.multiple_of(meta[e], MOE_ALIGN) for e in experts]
    cnts = [meta[N_EXPERTS + e] for e in experts]
    one_tile_each = cnts[0] <= MOE_TR
    for c in cnts[1:]:
        one_tile_each = jnp.logical_and(one_tile_each, c <= MOE_TR)

    @pl.when(one_tile_each)
    def _():
        tiles = [expert_tile(k, offs[k]) for k in range(MOE_EXPERTS_PER_STEP)]
        for k in range(MOE_EXPERTS_PER_STEP):
            store_tile(offs[k], *tiles[k])

    @pl.when(jnp.logical_not(one_tile_each))
    def _():
        for k in range(MOE_EXPERTS_PER_STEP):
            n_tiles = (cnts[k] + (MOE_TR - 1)) // MOE_TR

            @pl.loop(0, n_tiles)
            def _(i):
                r0 = pl.multiple_of(offs[k] + i * MOE_TR, MOE_ALIGN)
                store_tile(r0, *expert_tile(k, r0))

    @pl.when(s == pl.num_programs(1) - 1)
    def _():
        d1c = d1c_s[...]
        d2c = d2c_s[...]
        acc = jnp.zeros((tb, D_MODEL), F32)
        for c in range(pr // MOE_CH):
            pcol = _iota_f32((tb, MOE_CH), 1) + float(c * MOE_CH)
            pt = _one_or_zero(pcol == d1c, pcol == d2c).astype(BF16)
            acc = (acc + jnp.dot(pt, out_hi[c * MOE_CH:(c + 1) * MOE_CH, :], preferred_element_type=F32)
                   + jnp.dot(pt, out_lo[c * MOE_CH:(c + 1) * MOE_CH, :], preferred_element_type=F32))
        o_ref[...] = x1_ref[...] + gt2_ref[...] * acc


def _moe_sorted(h2, route, route_t, x1, gt2, wg_bf, wu_bf, wd_bf):
    rows = h2.shape[0]
    tb = MOE_TB
    pr = -(-(2 * tb + N_EXPERTS * (MOE_ALIGN - 1)) // MOE_CH) * MOE_CH
    eps = MOE_EXPERTS_PER_STEP
    return pl.pallas_call(
        _moe_sorted_kernel,
        grid=(rows // tb, N_EXPERTS // eps),
        in_specs=[pl.BlockSpec((tb, D_MODEL), lambda b, s: (b, 0)),
                  pl.BlockSpec((tb, ROUTER_LANES), lambda b, s: (b, 0)),
                  pl.BlockSpec((SUBLANES, tb), lambda b, s: (0, b)),
                  pl.BlockSpec((tb, D_MODEL), lambda b, s: (b, 0)),
                  pl.BlockSpec((1, D_MODEL), lambda b, s: (0, 0)),
                  pl.BlockSpec((eps, D_MODEL, D_EXPERT), lambda b, s: (s, 0, 0)),
                  pl.BlockSpec((eps, D_MODEL, D_EXPERT), lambda b, s: (s, 0, 0)),
                  pl.BlockSpec((eps, D_EXPERT, D_MODEL), lambda b, s: (s, 0, 0))],
        out_specs=pl.BlockSpec((tb, D_MODEL), lambda b, s: (b, 0)),
        out_shape=jax.ShapeDtypeStruct((rows, D_MODEL), F32),
        scratch_shapes=[pltpu.VMEM((pr + MOE_TR, D_MODEL), BF16), pltpu.VMEM((pr + MOE_TR, D_MODEL), BF16),
                        pltpu.VMEM((pr + MOE_TR, D_MODEL), BF16), pltpu.VMEM((pr + MOE_TR, 1), F32),
                        pltpu.VMEM((tb, 1), F32), pltpu.VMEM((tb, 1), F32),
                        pltpu.SMEM((2 * N_EXPERTS,), jnp.int32)],
        compiler_params=pltpu.CompilerParams(dimension_semantics=("parallel", "arbitrary"),
                                             vmem_limit_bytes=MOE_VMEM_LIMIT),
        name="moe_sorted",
    )(h2, route, route_t, x1, gt2, wg_bf, wu_bf, wd_bf)


def _rope_tables(pos):
    inv = 1.0 / (ROPE_THETA ** (jnp.arange(0, HEAD_DIM, 2, dtype=F32) / HEAD_DIM))
    half = HEAD_DIM // 2
    reps = LANES // half
    ang = pos.astype(F32)[:, None] * jnp.tile(inv, reps)[None, :]
    sign = jnp.tile(jnp.concatenate([-jnp.ones((half,), F32), jnp.ones((half,), F32)]), reps // 2)
    return jnp.cos(ang), jnp.sin(ang) * sign[None, :]


def _lambda_init(layer):
    return 0.8 - 0.6 * math.exp(-0.3 * layer)


def kernel(x_prompt, x_sample, cache_k, cache_v, state_conv, page_table, c_prompt, c_sample, w_ada, b_ada, g_norm1, g_norm2, w_in, g_qnorm, g_knorm, lam_q1, lam_k1, lam_q2, lam_k2, g_subln, w_dw, b_dw, g_ln_conv, b_ln_conv, w_out, w_router_group, b_router_group, w_router_expert, b_router_expert, w_gate_e, w_up_e, w_down_e):
    depth = w_ada.shape[0]
    assert depth == 1, "single-layer trunk"
    layer = 0
    batch_p, seq, _ = x_prompt.shape
    batch_s, n_new, _ = x_sample.shape
    assert batch_p == 1
    past = page_table.shape[1] * PAGE_SIZE
    n_pool = cache_k.shape[1]
    lam_init = _lambda_init(layer)
    row = lambda a: a[layer].reshape(1, -1)

    n_c = batch_p + batch_s
    n_c_pad = -(-n_c // SUBLANES) * SUBLANES
    c_all = jnp.concatenate([c_prompt, c_sample, jnp.zeros((n_c_pad - n_c, D_MODEL), F32)], axis=0)
    mod = _ada(c_all, w_ada[layer], b_ada[layer])
    mod_p = [mod[0:1, i * D_MODEL:(i + 1) * D_MODEL] for i in range(N_MOD)]
    mod_s = [jnp.repeat(mod[1:1 + batch_s, i * D_MODEL:(i + 1) * D_MODEL], n_new, axis=0) for i in range(N_MOD)]

    w_in_bf = w_in[layer].astype(BF16)
    w_out_bf = w_out[layer].astype(BF16)
    wg_bf = w_gate_e[layer].astype(BF16)
    wu_bf = w_up_e[layer].astype(BF16)
    wd_bf = w_down_e[layer].astype(BF16)
    gq = jnp.tile(g_qnorm[layer], QK_WIDTH // HEAD_DIM).reshape(1, -1)
    gk = jnp.tile(g_knorm[layer], QK_WIDTH // HEAD_DIM).reshape(1, -1)
    seg_id = jnp.arange(QK_WIDTH) // HEAD_DIM
    seg = (seg_id[:, None] == seg_id[None, :]).astype(BF16)
    w_r = jnp.concatenate([w_router_expert[layer], w_router_group[layer],
                           jnp.zeros((D_MODEL, ROUTER_LANES - N_EXPERTS - N_GROUPS), F32)], axis=1)
    wr_hi = w_r.astype(BF16)
    wr_lo = (w_r - wr_hi.astype(F32)).astype(BF16)
    b_r = jnp.concatenate([b_router_expert[layer], b_router_group[layer],
                           jnp.zeros((ROUTER_LANES - N_EXPERTS - N_GROUPS,), F32)]).reshape(1, -1)
    lam_p = [row(lam_q1), row(lam_k1), row(lam_q2), row(lam_k2)]
    gs = row(g_subln)
    conv_p = (w_dw[layer], row(b_dw), row(g_ln_conv), row(b_ln_conv))

    xp = x_prompt.reshape(seq, D_MODEL)
    cos_p, sin_p = _rope_tables(jnp.arange(seq))
    k_p, v_p, glu_p, q_bf, k_bf, v_bf = _in_proj(xp, mod_p[0], mod_p[1], row(g_norm1), w_in_bf, gq, gk,
                                                 cos_p, sin_p, seg, tm=1024)
    oc_p = _conv_prompt(glu_p, *conv_p, tm=512)
    oa_p = _attn_prompt(q_bf, k_bf, v_bf, lam_p, gs, lam_init, tq=1024, tk=1024)
    x1_p, h2_p, _, route_p = _out_router(xp, oa_p, oc_p, w_out_bf, mod_p[2], mod_p[3], mod_p[4], row(g_norm2),
                                         wr_hi, wr_lo, b_r, tm=1024)
    y_p = _moe_sorted(h2_p, route_p, jnp.transpose(route_p[:, :SUBLANES]), x1_p, mod_p[5], wg_bf, wu_bf, wd_bf)

    xs = x_sample.reshape(batch_s * n_new, D_MODEL)
    pos_s = past + (jnp.arange(batch_s * n_new) % n_new)
    cos_s, sin_s = _rope_tables(pos_s)
    rows_s = batch_s * n_new
    k_s, v_s, glu_s, qs_bf, ks_bf, vs_bf = _in_proj(xs, mod_s[0], mod_s[1], row(g_norm1), w_in_bf, gq, gk,
                                                    cos_s, sin_s, seg, tm=rows_s)
    xp_s = jnp.concatenate([state_conv[layer], glu_s.reshape(batch_s, n_new, CONV_CH)], axis=1)
    oc_s = _conv_sample(jnp.transpose(xp_s, (1, 0, 2)), *conv_p, n_new=n_new)
    oc_s = jnp.transpose(oc_s, (1, 0, 2)).reshape(rows_s, CONV_CH)
    tok3 = lambda a: a.reshape(batch_s, n_new, D_ATTN)
    pad_q = lambda a: jnp.pad(tok3(a), ((0, 0), (0, SUBLANES - n_new), (0, 0)))
    oa_s = _attn_sample(page_table, _block_diag_queries(tok3(qs_bf), SUBLANES), pad_q(k_s), pad_q(v_s),
                        jnp.transpose(cache_k[layer], (0, 2, 3, 4, 1)).reshape(n_pool, D_ATTN, PAGE_SIZE),
                        cache_v[layer].reshape(n_pool, PAGE_SIZE * N_HEADS, V_DIM), lam_p, gs, lam_init, n_new)
    oa_s = oa_s[:, :n_new].reshape(rows_s, D_ATTN)
    x1_s, h2_s, gates_s, _ = _out_router(xs, oa_s, oc_s, w_out_bf, mod_s[2], mod_s[3],
                                      mod_s[4], row(g_norm2), wr_hi, wr_lo, b_r, tm=rows_s)
    y_s = _moe(h2_s, gates_s, x1_s, mod_s[5], wg_bf, wu_bf, wd_bf, tm=rows_s)

    hshape = (N_HEADS, 2, HEAD_DIM)
    return (y_p.reshape(batch_p, seq, D_MODEL),
            y_s.reshape(batch_s, n_new, D_MODEL),
            k_p.reshape(depth, batch_p, seq, *hshape),
            v_p.reshape(depth, batch_p, seq, N_HEADS, V_DIM),
            glu_p[seq - (CONV_WIDTH - 1):].reshape(depth, batch_p, CONV_WIDTH - 1, CONV_CH),
            k_s.reshape(depth, batch_s, n_new, *hshape),
            v_s.reshape(depth, batch_s, n_new, N_HEADS, V_DIM),
            xp_s[:, n_new:].reshape(depth, batch_s, CONV_WIDTH - 1, CONV_CH))
```

```python
import functools
import math

import jax
import jax.numpy as jnp
from jax import lax
from jax.experimental import pallas as pl
from jax.experimental.pallas import tpu as pltpu

F32 = jnp.float32
BF16 = jnp.bfloat16

D_MODEL = 1024
HEAD_DIM = 64
V_DIM = 2 * HEAD_DIM
N_HEADS = (D_MODEL // 2) // V_DIM
D_ATTN = N_HEADS * V_DIM
QK_WIDTH = N_HEADS * 2 * HEAD_DIM
CONV_CH = D_MODEL - D_ATTN
CONV_WIDTH = 31
IN_COLS = 2 * QK_WIDTH + D_ATTN + 2 * CONV_CH
N_GROUPS = 4
EXPERTS_PER_GROUP = 8
N_EXPERTS = N_GROUPS * EXPERTS_PER_GROUP
D_EXPERT = D_MODEL // 4
ROPE_THETA = 10000.0
PAGE_SIZE = 128
EPS = 1e-6
N_MOD = 6
NEG_INF = -1e30
QK_SCALE_LOG2 = HEAD_DIM ** -0.5 * math.log2(math.e)

LANES = 128
SUBLANES = 8
VMEM_LIMIT = 48 * 1024 * 1024
HALO = 32
CONV_CHUNK = 64
PAGES_PER_STEP = 32
ROUTER_LANES = 128
MOE_TB = 1024
MOE_TR = 128
MOE_RANK_CH = 256
MOE_CH = 512
MOE_ALIGN = 16
MOE_EXPERTS_PER_STEP = 2
MOE_VMEM_LIMIT = 56 * 1024 * 1024


def _params(*sem):
    return pltpu.CompilerParams(dimension_semantics=sem, vmem_limit_bytes=VMEM_LIMIT)


def _nt_dot(a, b):
    return lax.dot_general(a, b, (((1,), (1,)), ((), ())), preferred_element_type=F32)


def _ada_kernel(c_ref, w_ref, b_ref, o_ref):
    c = c_ref[...]
    a = (c * jax.nn.sigmoid(c)).astype(BF16)
    o_ref[...] = jnp.dot(a, w_ref[...].astype(BF16), preferred_element_type=F32) + b_ref[...]


def _ada(c_all, w_ada, b_ada):
    rows = c_all.shape[0]
    return pl.pallas_call(
        _ada_kernel,
        grid=(N_MOD,),
        in_specs=[
            pl.BlockSpec((rows, D_MODEL), lambda j: (0, 0)),
            pl.BlockSpec((D_MODEL, D_MODEL), lambda j: (0, j)),
            pl.BlockSpec((1, D_MODEL), lambda j: (0, j)),
        ],
        out_specs=pl.BlockSpec((rows, D_MODEL), lambda j: (0, j)),
        out_shape=jax.ShapeDtypeStruct((rows, N_MOD * D_MODEL), F32),
        compiler_params=_params("arbitrary"),
        name="ada",
    )(c_all, w_ada, b_ada.reshape(1, -1))


def _rms_rows(x, g):
    return x * lax.rsqrt(jnp.mean(x * x, axis=-1, keepdims=True) + EPS) * g


def _qk_norm_rope(t, g, seg, cos, sin_signed, first_half):
    ms = jnp.dot((t * t).astype(BF16), seg, preferred_element_type=F32) * (1.0 / HEAD_DIM)
    tn = t * lax.rsqrt(ms + EPS) * g
    out = []
    for i in range(t.shape[1] // LANES):
        xs = tn[:, i * LANES:(i + 1) * LANES]
        rot = jnp.where(first_half, pltpu.roll(xs, LANES - HEAD_DIM // 2, 1), pltpu.roll(xs, HEAD_DIM // 2, 1))
        out.append(xs * cos + rot * sin_signed)
    return jnp.concatenate(out, axis=1)


def _in_proj_kernel(x_ref, sh_ref, sc_ref, g1_ref, w_ref, gq_ref, gk_ref, cos_ref, sin_ref, seg_ref,
                    k_out, v_out, glu_out, q_bf, k_bf, v_bf):
    x = x_ref[...]
    h = _rms_rows(x, g1_ref[...]) * (1.0 + sc_ref[...]) + sh_ref[...]
    proj = jnp.dot(h.astype(BF16), w_ref[...], preferred_element_type=F32)
    cos = cos_ref[...]
    sin = sin_ref[...]
    seg = seg_ref[...]
    lane = lax.broadcasted_iota(jnp.int32, cos.shape, 1)
    first_half = (lane % HEAD_DIM) < (HEAD_DIM // 2)
    q = _qk_norm_rope(proj[:, :QK_WIDTH], gq_ref[...], seg, cos, sin, first_half)
    k = _qk_norm_rope(proj[:, QK_WIDTH:2 * QK_WIDTH], gk_ref[...], seg, cos, sin, first_half)
    v = proj[:, 2 * QK_WIDTH:2 * QK_WIDTH + D_ATTN]
    a = proj[:, 2 * QK_WIDTH + D_ATTN:2 * QK_WIDTH + D_ATTN + CONV_CH]
    gate = proj[:, 2 * QK_WIDTH + D_ATTN + CONV_CH:]
    k_out[...] = k
    for hd in range(N_HEADS):
        v_out[pl.ds(hd, v.shape[0], stride=N_HEADS), :] = v[:, hd * V_DIM:(hd + 1) * V_DIM]
    glu_out[...] = a * jax.nn.sigmoid(gate)
    q_bf[...] = (q * QK_SCALE_LOG2).astype(BF16)
    k_bf[...] = k.astype(BF16)
    v_bf[...] = v.astype(BF16)


def _in_proj(x, sh, sc, g1, w_in_bf, gq, gk, cos, sin, seg, tm):
    rows = x.shape[0]
    per_row = sh.shape[0] != 1
    mod_spec = pl.BlockSpec((tm, D_MODEL), lambda i: (i, 0)) if per_row else pl.BlockSpec((1, D_MODEL), lambda i: (0, 0))
    const = lambda shape: pl.BlockSpec(shape, lambda i: (0, 0))
    row_spec = lambda width: pl.BlockSpec((tm, width), lambda i: (i, 0))
    f32_out = jax.ShapeDtypeStruct((rows, QK_WIDTH), F32)
    bf_out = jax.ShapeDtypeStruct((rows, QK_WIDTH), BF16)
    return pl.pallas_call(
        _in_proj_kernel,
        grid=(rows // tm,),
        in_specs=[row_spec(D_MODEL), mod_spec, mod_spec, const((1, D_MODEL)), const((D_MODEL, IN_COLS)),
                  const((1, QK_WIDTH)), const((1, QK_WIDTH)), row_spec(LANES), row_spec(LANES),
                  const((QK_WIDTH, QK_WIDTH))],
        out_specs=[row_spec(QK_WIDTH), pl.BlockSpec((tm * N_HEADS, V_DIM), lambda i: (i, 0))] + [row_spec(QK_WIDTH)] * 4,
        out_shape=[f32_out, jax.ShapeDtypeStruct((rows * N_HEADS, V_DIM), F32), f32_out, bf_out, bf_out, bf_out],
        compiler_params=_params("parallel"),
        name="in_proj",
    )(x, sh, sc, g1, w_in_bf, gq, gk, cos, sin, seg)


def _ln_swish(y, g, b):
    mu = jnp.mean(y, axis=-1, keepdims=True)
    d = y - mu
    var = jnp.mean(d * d, axis=-1, keepdims=True)
    z = d * lax.rsqrt(var + EPS) * g + b
    return z * jax.nn.sigmoid(z)


def _conv_prompt_kernel(glu_ref, w_ref, bdw_ref, g_ref, b_ref, o_ref, xpad, shifted):
    tm = glu_ref.shape[0]

    @pl.when(pl.program_id(0) == 0)
    def _():
        xpad[0:HALO, :] = jnp.zeros((HALO, CONV_CH), F32)

    xpad[HALO:HALO + tm, :] = glu_ref[...]
    base = HALO - (CONV_WIDTH - 1)
    for c in range(tm // CONV_CHUNK):
        r0 = c * CONV_CHUNK
        acc = jnp.zeros((CONV_CHUNK, CONV_CH), F32)
        for phase in range(SUBLANES):
            taps = [j for j in range(CONV_WIDTH) if (base + j) % SUBLANES == phase]
            reach = max(base + j - phase for j in taps) + CONV_CHUNK
            shifted[phase, 0:reach, :] = xpad[r0 + phase:r0 + phase + reach, :]
            for j in taps:
                a = base + j - phase
                acc = acc + shifted[phase, a:a + CONV_CHUNK, :] * w_ref[j:j + 1, :]
        y = acc + bdw_ref[...]
        o_ref[r0:r0 + CONV_CHUNK, :] = _ln_swish(y, g_ref[...], b_ref[...]).astype(BF16)
    xpad[0:HALO, :] = xpad[tm:tm + HALO, :]


def _conv_prompt(glu, w_dw, b_dw, g_ln, b_ln, tm):
    rows = glu.shape[0]
    const = lambda shape: pl.BlockSpec(shape, lambda i: (0, 0))
    return pl.pallas_call(
        _conv_prompt_kernel,
        grid=(rows // tm,),
        in_specs=[pl.BlockSpec((tm, CONV_CH), lambda i: (i, 0)), const((CONV_WIDTH, CONV_CH)),
                  const((1, CONV_CH)), const((1, CONV_CH)), const((1, CONV_CH))],
        out_specs=pl.BlockSpec((tm, CONV_CH), lambda i: (i, 0)),
        out_shape=jax.ShapeDtypeStruct((rows, CONV_CH), BF16),
        scratch_shapes=[pltpu.VMEM((tm + HALO, CONV_CH), F32),
                        pltpu.VMEM((SUBLANES, CONV_CHUNK + HALO, CONV_CH), F32)],
        compiler_params=_params("arbitrary"),
        name="conv_prompt",
    )(glu, w_dw, b_dw, g_ln, b_ln)


def _conv_sample_kernel(xp_ref, w_ref, bdw_ref, g_ref, b_ref, o_ref):
    n_new = o_ref.shape[0]
    for i in range(n_new):
        acc = jnp.zeros(o_ref.shape[1:], F32)
        for j in range(CONV_WIDTH):
            acc = acc + xp_ref[i + j] * w_ref[j:j + 1, :]
        o_ref[i] = _ln_swish(acc + bdw_ref[...], g_ref[...], b_ref[...]).astype(BF16)


def _conv_sample(xp_t, w_dw, b_dw, g_ln, b_ln, n_new):
    t, batch, ch = xp_t.shape
    const2 = lambda shape: pl.BlockSpec(shape, lambda i: (0, 0))
    return pl.pallas_call(
        _conv_sample_kernel,
        grid=(1,),
        in_specs=[pl.BlockSpec((t, batch, ch), lambda i: (0, 0, 0)), const2((CONV_WIDTH, ch)),
                  const2((1, ch)), const2((1, ch)), const2((1, ch))],
        out_specs=pl.BlockSpec((n_new, batch, ch), lambda i: (0, 0, 0)),
        out_shape=jax.ShapeDtypeStruct((n_new, batch, ch), BF16),
        compiler_params=_params("arbitrary"),
        name="conv_sample",
    )(xp_t, w_dw, b_dw, g_ln, b_ln)


def _lambda(lq1, lk1, lq2, lk2, lam_init):
    s1 = jnp.sum(lq1[...] * lk1[...], axis=-1, keepdims=True)
    s2 = jnp.sum(lq2[...] * lk2[...], axis=-1, keepdims=True)
    return jnp.exp(s1) - jnp.exp(s2) + lam_init


def _online_step(s, vb, m, l, acc):
    m_new = jnp.maximum(m, jnp.max(s, axis=-1, keepdims=True))
    alpha = jnp.exp2(m - m_new)
    e = jnp.exp2(s - m_new)
    l = alpha * l + jnp.sum(e, axis=-1, keepdims=True)
    acc = alpha * acc + jnp.dot(e.astype(BF16), vb, preferred_element_type=F32)
    return m_new, l, acc


def _attn_prompt_kernel(q_ref, k_ref, v_ref, lq1, lk1, lq2, lk2, gs_ref, o_ref, m_scr, l_scr, acc_scr, lp_scr,
                        kmax_scr, *, lam_init, tk):
    tq = q_ref.shape[0]
    qi = pl.program_id(1)
    q = q_ref[...]
    lane = lax.broadcasted_iota(jnp.int32, q.shape, 1)
    zero = jnp.zeros_like(q)
    q2 = jnp.concatenate([jnp.where(lane < HEAD_DIM, q, zero), jnp.where(lane >= HEAD_DIM, q, zero)], axis=0)

    @pl.when(qi == 0)
    def _():
        def chunk_max(c, carry):
            kb = k_ref[pl.ds(pl.multiple_of(c * tk, tk), tk), :].astype(F32)
            kk = kb * kb
            lane_k = lax.broadcasted_iota(jnp.int32, kk.shape, 1)
            n0 = jnp.sum(jnp.where(lane_k < HEAD_DIM, kk, 0.0), axis=-1, keepdims=True)
            n1 = jnp.sum(jnp.where(lane_k >= HEAD_DIM, kk, 0.0), axis=-1, keepdims=True)
            return (jnp.maximum(carry[0], jnp.max(n0, axis=0, keepdims=True)),
                    jnp.maximum(carry[1], jnp.max(n1, axis=0, keepdims=True)))

        k0, k1 = lax.fori_loop(0, k_ref.shape[0] // tk, chunk_max, (jnp.zeros((1, 1), F32), jnp.zeros((1, 1), F32)))
        kmax_scr[0:1, :] = jnp.broadcast_to(k0, (1, LANES))
        kmax_scr[1:2, :] = jnp.broadcast_to(k1, (1, LANES))

    qf = q2.astype(F32)
    kmax = jnp.concatenate([jnp.broadcast_to(kmax_scr[0:1, 0:1], (tq, 1)),
                            jnp.broadcast_to(kmax_scr[1:2, 0:1], (tq, 1))], axis=0)
    bound = jnp.sqrt(jnp.sum(qf * qf, axis=-1, keepdims=True) * kmax)

    def scores(j, masked):
        off = pl.multiple_of(j * tk, tk)
        s = _nt_dot(q2, k_ref[pl.ds(off, tk), :])
        if masked:
            row = lax.broadcasted_iota(jnp.int32, s.shape, 0) % tq + qi * tq
            col = lax.broadcasted_iota(jnp.int32, s.shape, 1) + off
            s = jnp.where(col <= row, s, NEG_INF)
        return s, v_ref[pl.ds(off, tk), :]

    def fold_lanes(e):
        part = e[:, 0:LANES]
        for t in range(1, e.shape[1] // LANES):
            part = part + e[:, t * LANES:(t + 1) * LANES]
        return part

    def fast(j, masked):
        if masked and tq == tk:
            return fast_diagonal(j)
        s, vb = scores(j, masked)
        e = jnp.exp2(s - bound)
        lp_scr[...] += fold_lanes(e)
        acc_scr[...] += jnp.dot(e.astype(BF16), vb, preferred_element_type=F32)

    def fast_diagonal(j):
        hq = tq // 2
        off = pl.multiple_of(j * tk, tk)
        s = _nt_dot(q2, k_ref[pl.ds(off, hq), :])
        row = lax.broadcasted_iota(jnp.int32, s.shape, 0) % tq
        col = lax.broadcasted_iota(jnp.int32, s.shape, 1)
        e = jnp.exp2(jnp.where(col <= row, s, NEG_INF) - bound)
        lp_scr[...] += fold_lanes(e)
        acc_scr[...] += jnp.dot(e.astype(BF16), v_ref[pl.ds(off, hq), :], preferred_element_type=F32)

        late = [slice(hq, tq), slice(tq + hq, 2 * tq)]
        q_late = jnp.concatenate([q2[r] for r in late], axis=0)
        b_late = jnp.concatenate([bound[r] for r in late], axis=0)
        s = _nt_dot(q_late, k_ref[pl.ds(pl.multiple_of(off + hq, hq), hq), :])
        row = lax.broadcasted_iota(jnp.int32, s.shape, 0) % hq
        col = lax.broadcasted_iota(jnp.int32, s.shape, 1)
        e = jnp.exp2(jnp.where(col <= row, s, NEG_INF) - b_late)
        part = fold_lanes(e)
        pv = jnp.dot(e.astype(BF16), v_ref[pl.ds(pl.multiple_of(off + hq, hq), hq), :], preferred_element_type=F32)
        for n, r in enumerate(late):
            lp_scr[r, :] += part[n * hq:(n + 1) * hq]
            acc_scr[r, :] += pv[n * hq:(n + 1) * hq]

    def exact(j, masked):
        s, vb = scores(j, masked)
        m, l, acc = _online_step(s, vb, m_scr[...], l_scr[...], acc_scr[...])
        m_scr[...] = m
        l_scr[...] = l
        acc_scr[...] = acc

    def finish(l):
        lam = _lambda(lq1, lk1, lq2, lk2, lam_init)
        acc = acc_scr[...]
        o = acc[:tq] / l[:tq] - lam * (acc[tq:] / l[tq:])
        o_ref[...] = (_rms_rows(o, gs_ref[...]) * (1.0 - lam_init)).astype(BF16)

    n_full = (qi * tq) // tk
    n_pairs = n_full // 2
    last = 2 * n_pairs
    lp_scr[...] = jnp.zeros(lp_scr.shape, F32)
    acc_scr[...] = jnp.zeros(acc_scr.shape, F32)

    @pl.loop(0, n_pairs)
    def _(p):
        fast(2 * p, False)
        fast(2 * p + 1, False)

    @pl.when(n_full == last)
    def _():
        fast(last, True)

    @pl.when(n_full != last)
    def _():
        fast(last, False)
        fast(last + 1, True)

    l_fast = jnp.sum(lp_scr[...], axis=-1, keepdims=True)
    trusted = jnp.logical_and(jnp.min(l_fast) > 2.0 ** -60, jnp.max(l_fast) < 2.0 ** 100)

    @pl.when(trusted)
    def _():
        finish(l_fast)

    @pl.when(jnp.logical_not(trusted))
    def _():
        m_scr[...] = jnp.full(m_scr.shape, NEG_INF, F32)
        l_scr[...] = jnp.zeros(l_scr.shape, F32)
        acc_scr[...] = jnp.zeros(acc_scr.shape, F32)

        @pl.loop(0, n_full)
        def _(j):
            exact(j, False)

        exact(n_full, True)
        finish(l_scr[...])


def _attn_prompt(q_bf, k_bf, v_bf, lam_p, g_subln, lam_init, tq, tk):
    s = q_bf.shape[0]
    assert tk % tq == 0 and s % tk == 0
    vec = lambda n: pl.BlockSpec((1, n), lambda h, i: (0, 0))
    return pl.pallas_call(
        functools.partial(_attn_prompt_kernel, lam_init=lam_init, tk=tk),
        grid=(N_HEADS, s // tq),
        in_specs=[pl.BlockSpec((tq, V_DIM), lambda h, i: (i, h)),
                  pl.BlockSpec((s, V_DIM), lambda h, i: (0, h)),
                  pl.BlockSpec((s, V_DIM), lambda h, i: (0, h)),
                  vec(HEAD_DIM), vec(HEAD_DIM), vec(HEAD_DIM), vec(HEAD_DIM), vec(V_DIM)],
        out_specs=pl.BlockSpec((tq, V_DIM), lambda h, i: (i, h)),
        out_shape=jax.ShapeDtypeStruct((s, D_ATTN), BF16),
        scratch_shapes=[pltpu.VMEM((2 * tq, 1), F32), pltpu.VMEM((2 * tq, 1), F32),
                        pltpu.VMEM((2 * tq, V_DIM), F32), pltpu.VMEM((2 * tq, LANES), F32),
                        pltpu.VMEM((SUBLANES, LANES), F32)],
        compiler_params=_params("arbitrary", "arbitrary"),
        name="attn_prompt",
    )(q_bf, k_bf, v_bf, *lam_p, g_subln)


def _attn_sample_kernel(pt_ref, qbd_ref, kn_ref, vn_ref, *rest, lam_init, n_new):
    del pt_ref
    npg = PAGES_PER_STEP
    kp = rest[:npg]
    vp = rest[npg:2 * npg]
    lq1, lk1, lq2, lk2, gs_ref, o_ref, m_s, l_s, acc_s = rest[2 * npg:]
    j = pl.program_id(1)
    n_rows = qbd_ref.shape[1]
    hrows = n_rows // N_HEADS
    qpad = hrows // 2

    @pl.when(j == 0)
    def _():
        m_s[...] = jnp.full(m_s.shape, NEG_INF, F32)
        l_s[...] = jnp.zeros(l_s.shape, F32)
        acc_s[...] = jnp.zeros(acc_s.shape, F32)

    qbd = qbd_ref[0]
    kc = jnp.concatenate([r[0] for r in kp], axis=1).astype(BF16)
    s = jnp.dot(qbd, kc, preferred_element_type=F32)
    m = m_s[...]
    m_new = jnp.maximum(m, jnp.max(s, axis=-1, keepdims=True))
    alpha = jnp.exp2(m - m_new)
    e = jnp.exp2(s - m_new)
    l = alpha * l_s[...] + jnp.sum(e, axis=-1, keepdims=True)
    e = e.astype(BF16)
    pv = []
    for h in range(N_HEADS):
        vh = jnp.concatenate([r[0, pl.ds(h, PAGE_SIZE, stride=N_HEADS), :] for r in vp], axis=0).astype(BF16)
        pv.append(jnp.dot(e[h * hrows:(h + 1) * hrows], vh, preferred_element_type=F32))
    acc = alpha * acc_s[...] + jnp.concatenate(pv, axis=0)
    m_s[...] = m_new
    l_s[...] = l
    acc_s[...] = acc

    @pl.when(j == pl.num_programs(1) - 1)
    def _():
        qf = qbd.astype(F32)
        kn = kn_ref[0].astype(BF16).astype(F32)
        vn = vn_ref[0].astype(BF16).astype(F32)
        qpos = lax.broadcasted_iota(jnp.int32, (n_rows, 1), 0) % qpad
        s_new = [jnp.where(t <= qpos, jnp.sum(qf * kn[t:t + 1, :], axis=-1, keepdims=True), NEG_INF)
                 for t in range(n_new)]
        m_f = m_new
        for s_t in s_new:
            m_f = jnp.maximum(m_f, s_t)
        a_f = jnp.exp2(m_new - m_f)
        l_f = a_f * l
        acc_f = a_f * acc
        e_new = [jnp.exp2(s_t - m_f) for s_t in s_new]
        for e_t in e_new:
            l_f = l_f + e_t
        lam = _lambda(lq1, lk1, lq2, lk2, lam_init)
        gs = gs_ref[...]
        outs = []
        for h in range(N_HEADS):
            a_h = acc_f[h * hrows:(h + 1) * hrows]
            for t in range(n_new):
                e_t = e_new[t][h * hrows:(h + 1) * hrows].astype(BF16).astype(F32)
                a_h = a_h + e_t * vn[t:t + 1, h * V_DIM:(h + 1) * V_DIM]
            o_h = a_h / l_f[h * hrows:(h + 1) * hrows]
            outs.append(_rms_rows(o_h[:qpad] - lam * o_h[qpad:], gs))
        o_ref[0] = (jnp.concatenate(outs, axis=1) * (1.0 - lam_init)).astype(BF16)


def _attn_sample(page_table, qbd, kn_pad, vn_pad, cache_kt, cache_v, lam_p, g_subln, lam_init, n_new):
    batch, n_rows, _ = qbd.shape
    qpad = kn_pad.shape[1]
    n_pages = page_table.shape[1]
    steps = n_pages // PAGES_PER_STEP
    pt_flat = page_table.reshape(-1)
    tok = lambda r: pl.BlockSpec((1, r, D_ATTN), lambda b, j, pt: (b, 0, 0))

    def page_spec(p, shape):
        return pl.BlockSpec((1,) + shape, lambda b, j, pt: (pt[b * n_pages + j * PAGES_PER_STEP + p], 0, 0))

    vec = lambda n: pl.BlockSpec((1, n), lambda b, j, pt: (0, 0))
    k_pages = [page_spec(p, (D_ATTN, PAGE_SIZE)) for p in range(PAGES_PER_STEP)]
    v_pages = [page_spec(p, (PAGE_SIZE * N_HEADS, V_DIM)) for p in range(PAGES_PER_STEP)]
    grid_spec = pltpu.PrefetchScalarGridSpec(
        num_scalar_prefetch=1,
        grid=(batch, steps),
        in_specs=[tok(n_rows), tok(qpad), tok(qpad)] + k_pages + v_pages
                 + [vec(HEAD_DIM), vec(HEAD_DIM), vec(HEAD_DIM), vec(HEAD_DIM), vec(V_DIM)],
        out_specs=tok(qpad),
        scratch_shapes=[pltpu.VMEM((n_rows, 1), F32), pltpu.VMEM((n_rows, 1), F32),
                        pltpu.VMEM((n_rows, V_DIM), F32)],
    )
    return pl.pallas_call(
        functools.partial(_attn_sample_kernel, lam_init=lam_init, n_new=n_new),
        grid_spec=grid_spec,
        out_shape=jax.ShapeDtypeStruct((batch, qpad, D_ATTN), BF16),
        compiler_params=_params("parallel", "arbitrary"),
        name="attn_sample",
    )(pt_flat, qbd, kn_pad, vn_pad, *([cache_kt] * PAGES_PER_STEP), *([cache_v] * PAGES_PER_STEP),
      *lam_p, g_subln)


def _block_diag_queries(q_bf, qpad):
    batch, n_new, _ = q_bf.shape
    qp = jnp.pad(q_bf, ((0, 0), (0, qpad - n_new), (0, 0)))
    qt = jnp.tile(qp, (1, 2 * N_HEADS, 1))
    chunk = jnp.arange(2 * N_HEADS * qpad) // qpad
    keep = (jnp.arange(D_ATTN)[None, :] // HEAD_DIM) == chunk[:, None]
    return jnp.where(keep[None], qt, jnp.zeros_like(qt))


def _lane_min_where(cond, lane, width):
    return jnp.min(jnp.where(cond, lane, width), axis=-1, keepdims=True)


def _out_router_kernel(x_ref, oa_ref, oc_ref, w_ref, gt1_ref, sh2_ref, sc2_ref, g2_ref,
                       wr_hi_ref, wr_lo_ref, br_ref, x1_out, h2_out, gates_out, route_out):
    mix = (jnp.dot(oa_ref[...], w_ref[0:D_ATTN, :], preferred_element_type=F32)
           + jnp.dot(oc_ref[...], w_ref[D_ATTN:, :], preferred_element_type=F32))
    x1 = x_ref[...] + gt1_ref[...] * mix
    x1_out[...] = x1
    h2 = _rms_rows(x1, g2_ref[...]) * (1.0 + sc2_ref[...]) + sh2_ref[...]
    h2_out[...] = h2.astype(BF16)

    h_hi = h2.astype(BF16)
    h_lo = (h2 - h_hi.astype(F32)).astype(BF16)
    logits = (jnp.dot(h_hi, wr_hi_ref[...], preferred_element_type=F32)
              + jnp.dot(h_hi, wr_lo_ref[...], preferred_element_type=F32)
              + jnp.dot(h_lo, wr_hi_ref[...], preferred_element_type=F32)) + br_ref[...]
    lane_i = lax.broadcasted_iota(jnp.int32, logits.shape, 1)
    lane = lane_i.astype(F32)
    lg = jnp.where(lane_i >= N_EXPERTS, jnp.where(lane_i < N_EXPERTS + N_GROUPS, logits, NEG_INF), NEG_INF)
    mg = jnp.max(lg, axis=-1, keepdims=True)
    pg_top = 1.0 / jnp.sum(jnp.exp(lg - mg), axis=-1, keepdims=True)
    g_idx = _lane_min_where(lg == mg, lane, float(ROUTER_LANES)) - N_EXPERTS
    lane_group = (lane_i // EXPERTS_PER_GROUP).astype(F32)
    le = jnp.where(lane_i < N_EXPERTS, jnp.where(lane_group == g_idx, logits, NEG_INF), NEG_INF)
    m1 = jnp.max(le, axis=-1, keepdims=True)
    i1 = _lane_min_where(le == m1, lane, float(ROUTER_LANES))
    le2 = jnp.where(lane == i1, NEG_INF, le)
    m2 = jnp.max(le2, axis=-1, keepdims=True)
    i2 = _lane_min_where(le2 == m2, lane, float(ROUTER_LANES))
    r = jnp.exp(m2 - m1)
    w1 = 1.0 / (1.0 + r)
    w2 = r / (1.0 + r)
    g1 = pg_top * w1
    g2 = pg_top * w2
    gates_out[...] = jnp.where(lane == i1, g1, 0.0) + jnp.where(lane == i2, g2, 0.0)
    route_out[...] = jnp.where(lane_i == 0, i1, jnp.where(lane_i == 1, i2, jnp.where(lane_i == 2, g1,
                               jnp.where(lane_i == 3, g2, 0.0))))


def _out_router(x, oa, oc, w_out_bf, gt1, sh2, sc2, g2, wr_hi, wr_lo, br, tm):
    rows = x.shape[0]
    per_row = gt1.shape[0] != 1
    mod_spec = pl.BlockSpec((tm, D_MODEL), lambda i: (i, 0)) if per_row else pl.BlockSpec((1, D_MODEL), lambda i: (0, 0))
    const = lambda shape: pl.BlockSpec(shape, lambda i: (0, 0))
    row_spec = lambda width: pl.BlockSpec((tm, width), lambda i: (i, 0))
    return pl.pallas_call(
        _out_router_kernel,
        grid=(rows // tm,),
        in_specs=[row_spec(D_MODEL), row_spec(D_ATTN), row_spec(CONV_CH), const((D_MODEL, D_MODEL)),
                  mod_spec, mod_spec, mod_spec, const((1, D_MODEL)),
                  const((D_MODEL, ROUTER_LANES)), const((D_MODEL, ROUTER_LANES)), const((1, ROUTER_LANES))],
        out_specs=[row_spec(D_MODEL), row_spec(D_MODEL), row_spec(ROUTER_LANES), row_spec(ROUTER_LANES)],
        out_shape=[jax.ShapeDtypeStruct((rows, D_MODEL), F32), jax.ShapeDtypeStruct((rows, D_MODEL), BF16),
                   jax.ShapeDtypeStruct((rows, ROUTER_LANES), F32), jax.ShapeDtypeStruct((rows, ROUTER_LANES), F32)],
        compiler_params=_params("parallel"),
        name="out_router",
    )(x, oa, oc, w_out_bf, gt1, sh2, sc2, g2, wr_hi, wr_lo, br)


def _moe_kernel(h_ref, gates_ref, x1_ref, gt2_ref, wg_ref, wu_ref, wd_ref, y_ref, acc):
    e = pl.program_id(1)

    @pl.when(e == 0)
    def _():
        acc[...] = jnp.zeros(acc.shape, F32)

    t = h_ref[...]
    g = jnp.dot(t, wg_ref[0], preferred_element_type=F32)
    u = jnp.dot(t, wu_ref[0], preferred_element_type=F32)
    he = (g * jax.nn.sigmoid(g)) * u
    out = jnp.dot(he.astype(BF16), wd_ref[0], preferred_element_type=F32)
    gates = gates_ref[...]
    lane = lax.broadcasted_iota(jnp.int32, gates.shape, 1)
    ge = jnp.sum(jnp.where(lane == e, gates, 0.0), axis=-1, keepdims=True)
    acc[...] += ge * out

    @pl.when(e == pl.num_programs(1) - 1)
    def _():
        y_ref[...] = x1_ref[...] + gt2_ref[...] * acc[...]


def _moe(h2, gates, x1, gt2, wg_bf, wu_bf, wd_bf, tm):
    rows = h2.shape[0]
    per_row = gt2.shape[0] != 1
    mod_spec = (pl.BlockSpec((tm, D_MODEL), lambda i, e: (i, 0)) if per_row
                else pl.BlockSpec((1, D_MODEL), lambda i, e: (0, 0)))
    row_spec = lambda width: pl.BlockSpec((tm, width), lambda i, e: (i, 0))
    return pl.pallas_call(
        _moe_kernel,
        grid=(rows // tm, N_EXPERTS),
        in_specs=[row_spec(D_MODEL), row_spec(ROUTER_LANES), row_spec(D_MODEL), mod_spec,
                  pl.BlockSpec((1, D_MODEL, D_EXPERT), lambda i, e: (e, 0, 0)),
                  pl.BlockSpec((1, D_MODEL, D_EXPERT), lambda i, e: (e, 0, 0)),
                  pl.BlockSpec((1, D_EXPERT, D_MODEL), lambda i, e: (e, 0, 0))],
        out_specs=row_spec(D_MODEL),
        out_shape=jax.ShapeDtypeStruct((rows, D_MODEL), F32),
        scratch_shapes=[pltpu.VMEM((tm, D_MODEL), F32)],
        compiler_params=_params("parallel", "arbitrary"),
        name="moe",
    )(h2, gates, x1, gt2, wg_bf, wu_bf, wd_bf)


def _iota_f32(shape, dim):
    return lax.broadcasted_iota(jnp.int32, shape, dim).astype(F32)


def _one_or_zero(a, b):
    return jnp.where(a, 1.0, jnp.where(b, 1.0, 0.0))


def _moe_sorted_kernel(h_ref, route_ref, route_t_ref, x1_ref, gt2_ref, wg_ref, wu_ref, wd_ref, o_ref,
                       xs, out_hi, out_lo, ws, d1c_s, d2c_s, meta):
    s = pl.program_id(1)
    tb = h_ref.shape[0]
    pr = xs.shape[0] - MOE_TR

    @pl.when(s == 0)
    def _():
        rt = route_t_ref[...]
        i1r, i2r, g1r, g2r = rt[0:1], rt[1:2], rt[2:3], rt[3:4]
        eio = _iota_f32((ROUTER_LANES, tb), 0)
        s1t = eio == i1r
        s2t = eio == i2r
        mt = _one_or_zero(s1t, s2t).astype(BF16)
        ranks = []
        for c in range(tb // MOE_RANK_CH):
            before = _iota_f32((tb, MOE_RANK_CH), 0) < _iota_f32((tb, MOE_RANK_CH), 1) + float(c * MOE_RANK_CH)
            ranks.append(jnp.dot(mt, jnp.where(before, 1.0, 0.0).astype(BF16), preferred_element_type=F32))
        rank_t = jnp.concatenate(ranks, axis=1)
        cnt_col = jnp.sum(mt.astype(F32), axis=1, keepdims=True)
        seg_col = jnp.floor((cnt_col + (MOE_ALIGN - 1.0)) * (1.0 / MOE_ALIGN))
        lower = jnp.where(_iota_f32((ROUTER_LANES, ROUTER_LANES), 1) < _iota_f32((ROUTER_LANES, ROUTER_LANES), 0),
                          1.0, 0.0).astype(BF16)
        off_col = MOE_ALIGN * jnp.dot(lower, jnp.broadcast_to(seg_col, (ROUTER_LANES, ROUTER_LANES)).astype(BF16),
                                      preferred_element_type=F32)[:, 0:1]
        dest_t = off_col + rank_t
        d1r = jnp.sum(jnp.where(s1t, dest_t, 0.0), axis=0, keepdims=True)
        d2r = jnp.sum(jnp.where(s2t, dest_t, 0.0), axis=0, keepdims=True)

        r = route_ref[...]
        lio = _iota_f32((tb, ROUTER_LANES), 1)
        s1 = lio == r[:, 0:1]
        s2 = lio == r[:, 1:2]
        m = _one_or_zero(s1, s2).astype(BF16)
        ranks = []
        for c in range(tb // MOE_RANK_CH):
            before = _iota_f32((MOE_RANK_CH, tb), 1) < _iota_f32((MOE_RANK_CH, tb), 0) + float(c * MOE_RANK_CH)
            ranks.append(jnp.dot(jnp.where(before, 1.0, 0.0).astype(BF16), m, preferred_element_type=F32))
        rank = jnp.concatenate(ranks, axis=0)
        cnt_row = jnp.sum(m.astype(F32), axis=0, keepdims=True)
        seg_row = jnp.floor((cnt_row + (MOE_ALIGN - 1.0)) * (1.0 / MOE_ALIGN))
        upper = jnp.where(_iota_f32((ROUTER_LANES, ROUTER_LANES), 0) < _iota_f32((ROUTER_LANES, ROUTER_LANES), 1),
                          1.0, 0.0).astype(BF16)
        off_row = MOE_ALIGN * jnp.dot(jnp.broadcast_to(seg_row, (SUBLANES, ROUTER_LANES)).astype(BF16), upper,
                                      preferred_element_type=F32)[0:1, :]
        dest = off_row + rank
        d1c_s[...] = jnp.sum(jnp.where(s1, dest, 0.0), axis=-1, keepdims=True)
        d2c_s[...] = jnp.sum(jnp.where(s2, dest, 0.0), axis=-1, keepdims=True)
        off_i = off_row.astype(jnp.int32)
        cnt_i = cnt_row.astype(jnp.int32)
        for e in range(N_EXPERTS):
            meta[e] = off_i[0, e]
            meta[N_EXPERTS + e] = cnt_i[0, e]

        x = h_ref[...]
        for c in range(pr // MOE_CH):
            prow = _iota_f32((MOE_CH, tb), 0) + float(c * MOE_CH)
            h1 = prow == d1r
            h2 = prow == d2r
            xs[c * MOE_CH:(c + 1) * MOE_CH, :] = jnp.dot(_one_or_zero(h1, h2).astype(BF16), x,
                                                       preferred_element_type=F32).astype(BF16)
            ws[c * MOE_CH:(c + 1) * MOE_CH, :] = jnp.sum(jnp.where(h1, g1r, 0.0) + jnp.where(h2, g2r, 0.0),
                                                       axis=-1, keepdims=True)
        xs[pr:pr + MOE_TR, :] = jnp.zeros((MOE_TR, D_MODEL), BF16)
        ws[pr:pr + MOE_TR, :] = jnp.zeros((MOE_TR, 1), F32)
        out_hi[...] = jnp.zeros(out_hi.shape, BF16)
        out_lo[...] = jnp.zeros(out_lo.shape, BF16)

    def expert_tile(k, r0):
        rows = xs[pl.ds(r0, MOE_TR), :]
        g = jnp.dot(rows, wg_ref[k], preferred_element_type=F32)
        u = jnp.dot(rows, wu_ref[k], preferred_element_type=F32)
        he = (g * jax.nn.sigmoid(g)) * u
        out = jnp.dot(he.astype(BF16), wd_ref[k], preferred_element_type=F32) * ws[pl.ds(r0, MOE_TR), :]
        hi = out.astype(BF16)
        return hi, (out - hi.astype(F32)).astype(BF16)

    def store_tile(r0, hi, lo):
        out_hi[pl.ds(r0, MOE_TR), :] = hi
        out_lo[pl.ds(r0, MOE_TR), :] = lo

    experts = [s * MOE_EXPERTS_PER_STEP + k for k in range(MOE_EXPERTS_PER_STEP)]
    offs = [pl.multiple_of(meta[e], MOE_ALIGN) for e in experts]
    cnts = [meta[N_EXPERTS + e] for e in experts]
    one_tile_each = cnts[0] <= MOE_TR
    for c in cnts[1:]:
        one_tile_each = jnp.logical_and(one_tile_each, c <= MOE_TR)

    @pl.when(one_tile_each)
    def _():
        tiles = [expert_tile(k, offs[k]) for k in range(MOE_EXPERTS_PER_STEP)]
        for k in range(MOE_EXPERTS_PER_STEP):
            store_tile(offs[k], *tiles[k])

    @pl.when(jnp.logical_not(one_tile_each))
    def _():
        for k in range(MOE_EXPERTS_PER_STEP):
            n_tiles = (cnts[k] + (MOE_TR - 1)) // MOE_TR

            @pl.loop(0, n_tiles)
            def _(i):
                r0 = pl.multiple_of(offs[k] + i * MOE_TR, MOE_ALIGN)
                store_tile(r0, *expert_tile(k, r0))

    @pl.when(s == pl.num_programs(1) - 1)
    def _():
        d1c = d1c_s[...]
        d2c = d2c_s[...]
        acc = jnp.zeros((tb, D_MODEL), F32)
        for c in range(pr // MOE_CH):
            pcol = _iota_f32((tb, MOE_CH), 1) + float(c * MOE_CH)
            pt = _one_or_zero(pcol == d1c, pcol == d2c).astype(BF16)
            acc = (acc + jnp.dot(pt, out_hi[c * MOE_CH:(c + 1) * MOE_CH, :], preferred_element_type=F32)
                   + jnp.dot(pt, out_lo[c * MOE_CH:(c + 1) * MOE_CH, :], preferred_element_type=F32))
        o_ref[...] = x1_ref[...] + gt2_ref[...] * acc


def _moe_sorted(h2, route, route_t, x1, gt2, wg_bf, wu_bf, wd_bf):
    rows = h2.shape[0]
    tb = MOE_TB
    pr = -(-(2 * tb + N_EXPERTS * (MOE_ALIGN - 1)) // MOE_CH) * MOE_CH
    eps = MOE_EXPERTS_PER_STEP
    return pl.pallas_call(
        _moe_sorted_kernel,
        grid=(rows // tb, N_EXPERTS // eps),
        in_specs=[pl.BlockSpec((tb, D_MODEL), lambda b, s: (b, 0)),
                  pl.BlockSpec((tb, ROUTER_LANES), lambda b, s: (b, 0)),
                  pl.BlockSpec((SUBLANES, tb), lambda b, s: (0, b)),
                  pl.BlockSpec((tb, D_MODEL), lambda b, s: (b, 0)),
                  pl.BlockSpec((1, D_MODEL), lambda b, s: (0, 0)),
                  pl.BlockSpec((eps, D_MODEL, D_EXPERT), lambda b, s: (s, 0, 0)),
                  pl.BlockSpec((eps, D_MODEL, D_EXPERT), lambda b, s: (s, 0, 0)),
                  pl.BlockSpec((eps, D_EXPERT, D_MODEL), lambda b, s: (s, 0, 0))],
        out_specs=pl.BlockSpec((tb, D_MODEL), lambda b, s: (b, 0)),
        out_shape=jax.ShapeDtypeStruct((rows, D_MODEL), F32),
        scratch_shapes=[pltpu.VMEM((pr + MOE_TR, D_MODEL), BF16), pltpu.VMEM((pr + MOE_TR, D_MODEL), BF16),
                        pltpu.VMEM((pr + MOE_TR, D_MODEL), BF16), pltpu.VMEM((pr + MOE_TR, 1), F32),
                        pltpu.VMEM((tb, 1), F32), pltpu.VMEM((tb, 1), F32),
                        pltpu.SMEM((2 * N_EXPERTS,), jnp.int32)],
        compiler_params=pltpu.CompilerParams(dimension_semantics=("parallel", "arbitrary"),
                                             vmem_limit_bytes=MOE_VMEM_LIMIT),
        name="moe_sorted",
    )(h2, route, route_t, x1, gt2, wg_bf, wu_bf, wd_bf)


def _rope_tables(pos):
    inv = 1.0 / (ROPE_THETA ** (jnp.arange(0, HEAD_DIM, 2, dtype=F32) / HEAD_DIM))
    half = HEAD_DIM // 2
    reps = LANES // half
    ang = pos.astype(F32)[:, None] * jnp.tile(inv, reps)[None, :]
    sign = jnp.tile(jnp.concatenate([-jnp.ones((half,), F32), jnp.ones((half,), F32)]), reps // 2)
    return jnp.cos(ang), jnp.sin(ang) * sign[None, :]


def _lambda_init(layer):
    return 0.8 - 0.6 * math.exp(-0.3 * layer)


def kernel(x_prompt, x_sample, cache_k, cache_v, state_conv, page_table, c_prompt, c_sample, w_ada, b_ada, g_norm1, g_norm2, w_in, g_qnorm, g_knorm, lam_q1, lam_k1, lam_q2, lam_k2, g_subln, w_dw, b_dw, g_ln_conv, b_ln_conv, w_out, w_router_group, b_router_group, w_router_expert, b_router_expert, w_gate_e, w_up_e, w_down_e):
    depth = w_ada.shape[0]
    assert depth == 1, "single-layer trunk"
    layer = 0
    batch_p, seq, _ = x_prompt.shape
    batch_s, n_new, _ = x_sample.shape
    assert batch_p == 1
    past = page_table.shape[1] * PAGE_SIZE
    n_pool = cache_k.shape[1]
    lam_init = _lambda_init(layer)
    row = lambda a: a[layer].reshape(1, -1)

    n_c = batch_p + batch_s
    n_c_pad = -(-n_c // SUBLANES) * SUBLANES
    c_all = jnp.concatenate([c_prompt, c_sample, jnp.zeros((n_c_pad - n_c, D_MODEL), F32)], axis=0)
    mod = _ada(c_all, w_ada[layer], b_ada[layer])
    mod_p = [mod[0:1, i * D_MODEL:(i + 1) * D_MODEL] for i in range(N_MOD)]
    mod_s = [jnp.repeat(mod[1:1 + batch_s, i * D_MODEL:(i + 1) * D_MODEL], n_new, axis=0) for i in range(N_MOD)]

    w_in_bf = w_in[layer].astype(BF16)
    w_out_bf = w_out[layer].astype(BF16)
    wg_bf = w_gate_e[layer].astype(BF16)
    wu_bf = w_up_e[layer].astype(BF16)
    wd_bf = w_down_e[layer].astype(BF16)
    gq = jnp.tile(g_qnorm[layer], QK_WIDTH // HEAD_DIM).reshape(1, -1)
    gk = jnp.tile(g_knorm[layer], QK_WIDTH // HEAD_DIM).reshape(1, -1)
    seg_id = jnp.arange(QK_WIDTH) // HEAD_DIM
    seg = (seg_id[:, None] == seg_id[None, :]).astype(BF16)
    w_r = jnp.concatenate([w_router_expert[layer], w_router_group[layer],
                           jnp.zeros((D_MODEL, ROUTER_LANES - N_EXPERTS - N_GROUPS), F32)], axis=1)
    wr_hi = w_r.astype(BF16)
    wr_lo = (w_r - wr_hi.astype(F32)).astype(BF16)
    b_r = jnp.concatenate([b_router_expert[layer], b_router_group[layer],
                           jnp.zeros((ROUTER_LANES - N_EXPERTS - N_GROUPS,), F32)]).reshape(1, -1)
    lam_p = [row(lam_q1), row(lam_k1), row(lam_q2), row(lam_k2)]
    gs = row(g_subln)
    conv_p = (w_dw[layer], row(b_dw), row(g_ln_conv), row(b_ln_conv))

    xp = x_prompt.reshape(seq, D_MODEL)
    cos_p, sin_p = _rope_tables(jnp.arange(seq))
    k_p, v_p, glu_p, q_bf, k_bf, v_bf = _in_proj(xp, mod_p[0], mod_p[1], row(g_norm1), w_in_bf, gq, gk,
                                                 cos_p, sin_p, seg, tm=1024)
    oc_p = _conv_prompt(glu_p, *conv_p, tm=512)
    oa_p = _attn_prompt(q_bf, k_bf, v_bf, lam_p, gs, lam_init, tq=1024, tk=1024)
    x1_p, h2_p, _, route_p = _out_router(xp, oa_p, oc_p, w_out_bf, mod_p[2], mod_p[3], mod_p[4], row(g_norm2),
                                         wr_hi, wr_lo, b_r, tm=1024)
    y_p = _moe_sorted(h2_p, route_p, jnp.transpose(route_p[:, :SUBLANES]), x1_p, mod_p[5], wg_bf, wu_bf, wd_bf)

    xs = x_sample.reshape(batch_s * n_new, D_MODEL)
    pos_s = past + (jnp.arange(batch_s * n_new) % n_new)
    cos_s, sin_s = _rope_tables(pos_s)
    rows_s = batch_s * n_new
    k_s, v_s, glu_s, qs_bf, ks_bf, vs_bf = _in_proj(xs, mod_s[0], mod_s[1], row(g_norm1), w_in_bf, gq, gk,
                                                    cos_s, sin_s, seg, tm=rows_s)
    xp_s = jnp.concatenate([state_conv[layer], glu_s.reshape(batch_s, n_new, CONV_CH)], axis=1)
    oc_s = _conv_sample(jnp.transpose(xp_s, (1, 0, 2)), *conv_p, n_new=n_new)
    oc_s = jnp.transpose(oc_s, (1, 0, 2)).reshape(rows_s, CONV_CH)
    tok3 = lambda a: a.reshape(batch_s, n_new, D_ATTN)
    pad_q = lambda a: jnp.pad(tok3(a), ((0, 0), (0, SUBLANES - n_new), (0, 0)))
    oa_s = _attn_sample(page_table, _block_diag_queries(tok3(qs_bf), SUBLANES), pad_q(k_s), pad_q(v_s),
                        jnp.transpose(cache_k[layer], (0, 2, 3, 4, 1)).reshape(n_pool, D_ATTN, PAGE_SIZE),
                        cache_v[layer].reshape(n_pool, PAGE_SIZE * N_HEADS, V_DIM), lam_p, gs, lam_init, n_new)
    oa_s = oa_s[:, :n_new].reshape(rows_s, D_ATTN)
    x1_s, h2_s, gates_s, _ = _out_router(xs, oa_s, oc_s, w_out_bf, mod_s[2], mod_s[3],
                                      mod_s[4], row(g_norm2), wr_hi, wr_lo, b_r, tm=rows_s)
    y_s = _moe(h2_s, gates_s, x1_s, mod_s[5], wg_bf, wu_bf, wd_bf, tm=rows_s)

    hshape = (N_HEADS, 2, HEAD_DIM)
    return (y_p.reshape(batch_p, seq, D_MODEL),
            y_s.reshape(batch_s, n_new, D_MODEL),
            k_p.reshape(depth, batch_p, seq, *hshape),
            v_p.reshape(depth, batch_p, seq, N_HEADS, V_DIM),
            glu_p[seq - (CONV_WIDTH - 1):].reshape(depth, batch_p, CONV_WIDTH - 1, CONV_CH),
            k_s.reshape(depth, batch_s, n_new, *hshape),
            v_s.reshape(depth, batch_s, n_new, N_HEADS, V_DIM),
            xp_s[:, n_new:].reshape(depth, batch_s, CONV_WIDTH - 1, CONV_CH))
```
